```python
import math
import jax, jax.numpy as jnp
from jax import lax
import numpy as np

D_MODEL = 1024
BATCH = 2
SEQ = 8192
DEPTH = 1
DEC_BATCH = 128
DEC_SEQ = 4
PAST_LEN = 16384
PAGE_SIZE = 128

D_MIX = D_MODEL
MLA_V = 64
MLA_HEADS = (D_MIX // 2) // MLA_V
MLA_NOPE = 64
MLA_ROPE = 32
MLA_QK = MLA_NOPE + MLA_ROPE
Q_LORA = 256
KV_LORA = 128
MLA_ROW = KV_LORA + MLA_ROPE
DIFF_D = 64
DIFF_HD = 2 * DIFF_D
DIFF_HEADS = (D_MIX // 2) // DIFF_HD
DIFF_ROT = DIFF_D // 4
D_FF = 2816
ROPE_THETA = 500000.0
Q_BLOCK = 128
EPS = 1e-6
NEG = -1e30
IN_COLS = Q_LORA + KV_LORA + MLA_ROPE + 3 * DIFF_HEADS * DIFF_HD

kernel_name = "hymba_mla_diffattn_macaron_step"

F32 = jnp.float32


def rmsnorm(x, g):
    xf = x.astype(F32)
    y = xf * lax.rsqrt(jnp.mean(xf * xf, axis=-1, keepdims=True) + EPS)
    return (y * g.astype(F32)).astype(x.dtype)


def rope(x, pos):
    r = x.shape[-1]
    half = r // 2
    inv = jnp.power(jnp.float32(ROPE_THETA), -jnp.arange(half, dtype=F32) * (2.0 / r))
    ang = pos.astype(F32)[:, None] * inv[None, :]
    cos = jnp.cos(ang)[None, :, None, :]
    sin = jnp.sin(ang)[None, :, None, :]
    xf = x.astype(F32)
    x1, x2 = xf[..., :half], xf[..., half:]
    return jnp.concatenate([x1 * cos - x2 * sin, x2 * cos + x1 * sin], axis=-1).astype(x.dtype)


def partial_rope(x, pos):
    return jnp.concatenate([rope(x[..., :DIFF_ROT], pos), x[..., DIFF_ROT:]], axis=-1)


def swiglu(h, wg, wu, wd):
    return (jax.nn.silu(h @ wg) * (h @ wu)) @ wd


def project(h, pos, lw):
    b, t, _ = h.shape
    z = h @ lw['w_in']
    o1 = Q_LORA
    o2 = o1 + KV_LORA
    o3 = o2 + MLA_ROPE
    o4 = o3 + DIFF_HEADS * DIFF_HD
    o5 = o4 + DIFF_HEADS * DIFF_HD
    c_q, c_kv, k_r, dq, dk, dv = jnp.split(z, [o1, o2, o3, o4, o5], axis=-1)
    c_q = rmsnorm(c_q, lw['mla_q_norm'])
    q = jnp.einsum('btc,chd->bthd', c_q, lw['w_uq'])
    q = jnp.concatenate([q[..., :MLA_NOPE], rope(q[..., MLA_NOPE:], pos)], axis=-1)
    q = rmsnorm(q, lw['mla_q_gain'])
    c_kv = rmsnorm(c_kv, lw['mla_kv_norm'])
    k_r = rope(k_r[:, :, None, :], pos)[:, :, 0, :]
    mla_row = jnp.concatenate([c_kv, k_r], axis=-1)
    dq = partial_rope(rmsnorm(dq.reshape(b, t, DIFF_HEADS * 2, DIFF_D), lw['diff_q_gain']), pos)
    dk = partial_rope(rmsnorm(dk.reshape(b, t, DIFF_HEADS * 2, DIFF_D), lw['diff_k_gain']), pos)
    dq = dq.reshape(b, t, DIFF_HEADS, 2, DIFF_D)
    dk = dk.reshape(b, t, DIFF_HEADS, DIFF_HD)
    dv = dv.reshape(b, t, DIFF_HEADS, DIFF_HD)
    return q, mla_row, dq, dk, dv


def mla_expand(row, lw):
    c = row[..., :KV_LORA]
    kr = row[..., KV_LORA:]
    kn = jnp.einsum('bsc,chd->bshd', c, lw['w_uk'])
    k = jnp.concatenate([kn, jnp.broadcast_to(kr[:, :, None, :], kn.shape[:3] + (MLA_ROPE,))], axis=-1)
    k = rmsnorm(k, lw['mla_k_gain'])
    v = jnp.einsum('bsc,chd->bshd', c, lw['w_uv'])
    return k, v


def mla_scores(q, k):
    return jnp.einsum('bthd,bshd->bhts', q, k).astype(F32) * (MLA_QK ** -0.5)


def diff_scores(q5, krow):
    k5 = krow.reshape(krow.shape[:3] + (2, DIFF_D))
    return jnp.einsum('bthmd,bshmd->bhmts', q5, k5).astype(F32) * (DIFF_D ** -0.5)


def online_update(state, s, v, spec):
    m, l, acc = state
    m_new = jnp.maximum(m, jnp.max(s, axis=-1))
    corr = jnp.exp(m - m_new)
    p = jnp.exp(s - m_new[..., None])
    l = l * corr + jnp.sum(p, axis=-1)
    acc = acc * corr[..., None] + jnp.einsum(spec, p, v.astype(F32))
    return (m_new, l, acc)


def combine(o_mla, o_diff, lam, lam_init, lw, dtype):
    b, t = o_mla.shape[:2]
    d = o_diff[:, :, :, 0] - lam * o_diff[:, :, :, 1]
    d = rmsnorm(d, lw['diff_sub_gain']) * (1.0 - lam_init)
    cat = jnp.concatenate([o_mla.reshape(b, t, -1), d.reshape(b, t, -1)], axis=-1).astype(dtype)
    return cat @ lw['w_o']


def mix_prompt(h, pos, lw, lam, lam_init):
    b, s_len, _ = h.shape
    q, row, dq, dk, dv = project(h, pos, lw)
    k, v = mla_expand(row, lw)
    nqb = s_len // Q_BLOCK

    def block(i):
        start = i * Q_BLOCK
        qb = lax.dynamic_slice_in_dim(q, start, Q_BLOCK, axis=1)
        dqb = lax.dynamic_slice_in_dim(dq, start, Q_BLOCK, axis=1)
        qpos = start + jnp.arange(Q_BLOCK, dtype=jnp.int32)
        mask = pos[None, :] <= qpos[:, None]
        pm = jax.nn.softmax(jnp.where(mask, mla_scores(qb, k), NEG), axis=-1)
        om = jnp.einsum('bhts,bshd->bthd', pm, v.astype(F32))
        pd = jax.nn.softmax(jnp.where(mask, diff_scores(dqb, dk), NEG), axis=-1)
        od = jnp.einsum('bhmts,bshe->bthme', pd, dv.astype(F32))
        return om, od

    om, od = lax.map(block, jnp.arange(nqb, dtype=jnp.int32))
    om = jnp.moveaxis(om, 0, 1).reshape(b, s_len, MLA_HEADS, MLA_V)
    od = jnp.moveaxis(od, 0, 1).reshape(b, s_len, DIFF_HEADS, 2, DIFF_HD)
    y = combine(om, od, lam, lam_init, lw, h.dtype)
    return y, row, dk, dv


def mix_sample(h, pos, page_table, cache_mla, cache_diff_k, cache_diff_v, layer, lw, lam, lam_init):
    b, t, _ = h.shape
    q, row, dq, dk, dv = project(h, pos, lw)
    init_m = (jnp.full((b, MLA_HEADS, t), NEG, F32), jnp.zeros((b, MLA_HEADS, t), F32),
              jnp.zeros((b, MLA_HEADS, t, MLA_V), F32))
    init_d = (jnp.full((b, DIFF_HEADS, 2, t), NEG, F32), jnp.zeros((b, DIFF_HEADS, 2, t), F32),
              jnp.zeros((b, DIFF_HEADS, 2, t, DIFF_HD), F32))

    def step(carry, pages):
        sm, sd = carry
        prow = cache_mla[layer, pages]
        pk, pv = mla_expand(prow, lw)
        sm = online_update(sm, mla_scores(q, pk), pv, 'bhts,bshd->bhtd')
        sd = online_update(sd, diff_scores(dq, cache_diff_k[layer, pages]),
                           cache_diff_v[layer, pages], 'bhmts,bshe->bhmte')
        return (sm, sd), None

    (sm, sd), _ = lax.scan(step, (init_m, init_d), page_table.T)
    mask = jnp.tril(jnp.ones((t, t), dtype=bool))
    kn, vn = mla_expand(row, lw)
    sm = online_update(sm, jnp.where(mask, mla_scores(q, kn), NEG), vn, 'bhts,bshd->bhtd')
    sd = online_update(sd, jnp.where(mask, diff_scores(dq, dk), NEG), dv, 'bhmts,bshe->bhmte')
    om = jnp.transpose(sm[2] / sm[1][..., None], (0, 2, 1, 3))
    od = jnp.transpose(sd[2] / sd[1][..., None], (0, 3, 1, 2, 4))
    y = combine(om, od, lam, lam_init, lw, h.dtype)
    return y, row, dk, dv


def macaron_layer(x, mixer, lw):
    x = x + 0.5 * swiglu(rmsnorm(x, lw['ffn1_norm']), lw['ffn1_w_gate'], lw['ffn1_w_up'], lw['ffn1_w_down'])
    y, row, dk, dv = mixer(rmsnorm(x, lw['mix_norm']))
    x = x + y
    x = x + 0.5 * swiglu(rmsnorm(x, lw['ffn2_norm']), lw['ffn2_w_gate'], lw['ffn2_w_up'], lw['ffn2_w_down'])
    return x, row, dk, dv


def setup_inputs(seed: int = 0) -> dict:
    key = jax.random.key(seed)
    ks = iter(jax.random.split(key, 40))
    n_pages = PAST_LEN // PAGE_SIZE
    n_pool = (DEC_BATCH * n_pages * 5) // 4

    def nrm(shape, scale):
        return jax.random.normal(next(ks), shape, F32) * scale

    def gain(shape):
        return 1.0 + 0.02 * jax.random.normal(next(ks), shape, F32)

    x_prompt = nrm((BATCH, SEQ, D_MODEL), 1.0)
    x_sample = nrm((DEC_BATCH, DEC_SEQ, D_MODEL), 1.0)
    cache_mla = nrm((DEPTH, n_pool, PAGE_SIZE, MLA_ROW), 1.0)
    cache_diff_k = nrm((DEPTH, n_pool, PAGE_SIZE, DIFF_HEADS, DIFF_HD), 1.0)
    cache_diff_v = nrm((DEPTH, n_pool, PAGE_SIZE, DIFF_HEADS, DIFF_HD), 1.0)
    perm = jax.random.permutation(next(ks), n_pool)[:DEC_BATCH * n_pages]
    page_table = perm.reshape(DEC_BATCH, n_pages).astype(jnp.int32)
    return {
        "x_prompt": x_prompt,
        "x_sample": x_sample,
        "cache_mla": cache_mla,
        "cache_diff_k": cache_diff_k,
        "cache_diff_v": cache_diff_v,
        "page_table": page_table,
        "ffn1_norm": gain((DEPTH, D_MODEL)),
        "ffn1_w_gate": nrm((DEPTH, D_MODEL, D_FF), D_MODEL ** -0.5),
        "ffn1_w_up": nrm((DEPTH, D_MODEL, D_FF), D_MODEL ** -0.5),
        "ffn1_w_down": nrm((DEPTH, D_FF, D_MODEL), D_FF ** -0.5),
        "mix_norm": gain((DEPTH, D_MODEL)),
        "w_in": nrm((DEPTH, D_MODEL, IN_COLS), D_MODEL ** -0.5),
        "mla_q_norm": gain((DEPTH, Q_LORA)),
        "mla_kv_norm": gain((DEPTH, KV_LORA)),
        "w_uq": nrm((DEPTH, Q_LORA, MLA_HEADS, MLA_QK), Q_LORA ** -0.5),
        "w_uk": nrm((DEPTH, KV_LORA, MLA_HEADS, MLA_NOPE), KV_LORA ** -0.5),
        "w_uv": nrm((DEPTH, KV_LORA, MLA_HEADS, MLA_V), KV_LORA ** -0.5),
        "mla_q_gain": gain((DEPTH, MLA_QK)),
        "mla_k_gain": gain((DEPTH, MLA_QK)),
        "diff_q_gain": gain((DEPTH, DIFF_D)),
        "diff_k_gain": gain((DEPTH, DIFF_D)),
        "diff_lq1": nrm((DEPTH, DIFF_D), 0.1),
        "diff_lk1": nrm((DEPTH, DIFF_D), 0.1),
        "diff_lq2": nrm((DEPTH, DIFF_D), 0.1),
        "diff_lk2": nrm((DEPTH, DIFF_D), 0.1),
        "diff_sub_gain": gain((DEPTH, DIFF_HD)),
        "w_o": nrm((DEPTH, D_MIX, D_MODEL), D_MIX ** -0.5),
        "ffn2_norm": gain((DEPTH, D_MODEL)),
        "ffn2_w_gate": nrm((DEPTH, D_MODEL, D_FF), D_MODEL ** -0.5),
        "ffn2_w_up": nrm((DEPTH, D_MODEL, D_FF), D_MODEL ** -0.5),
        "ffn2_w_down": nrm((DEPTH, D_FF, D_MODEL), D_FF ** -0.5),
    }


def reference(x_prompt, x_sample, cache_mla, cache_diff_k, cache_diff_v, page_table,
              ffn1_norm, ffn1_w_gate, ffn1_w_up, ffn1_w_down, mix_norm, w_in,
              mla_q_norm, mla_kv_norm, w_uq, w_uk, w_uv, mla_q_gain, mla_k_gain,
              diff_q_gain, diff_k_gain, diff_lq1, diff_lk1, diff_lq2, diff_lk2, diff_sub_gain,
              w_o, ffn2_norm, ffn2_w_gate, ffn2_w_up, ffn2_w_down):
    past_len = page_table.shape[1] * PAGE_SIZE
    pos_p = jnp.arange(x_prompt.shape[1], dtype=jnp.int32)
    pos_s = past_len + jnp.arange(x_sample.shape[1], dtype=jnp.int32)
    yp, ys = x_prompt, x_sample
    mla_p, dk_p, dv_p, mla_s, dk_s, dv_s = [], [], [], [], [], []
    for l in range(DEPTH):
        lw = {
            'ffn1_norm': ffn1_norm[l], 'ffn1_w_gate': ffn1_w_gate[l], 'ffn1_w_up': ffn1_w_up[l],
            'ffn1_w_down': ffn1_w_down[l], 'mix_norm': mix_norm[l], 'w_in': w_in[l],
            'mla_q_norm': mla_q_norm[l], 'mla_kv_norm': mla_kv_norm[l], 'w_uq': w_uq[l],
            'w_uk': w_uk[l], 'w_uv': w_uv[l], 'mla_q_gain': mla_q_gain[l], 'mla_k_gain': mla_k_gain[l],
            'diff_q_gain': diff_q_gain[l], 'diff_k_gain': diff_k_gain[l],
            'diff_sub_gain': diff_sub_gain[l], 'w_o': w_o[l], 'ffn2_norm': ffn2_norm[l],
            'ffn2_w_gate': ffn2_w_gate[l], 'ffn2_w_up': ffn2_w_up[l], 'ffn2_w_down': ffn2_w_down[l],
        }
        lam_init = 0.8 - 0.6 * math.exp(-0.3 * l)
        lam = (jnp.exp(jnp.sum(diff_lq1[l].astype(F32) * diff_lk1[l].astype(F32)))
               - jnp.exp(jnp.sum(diff_lq2[l].astype(F32) * diff_lk2[l].astype(F32))) + lam_init)
        yp, r_p, k_p, v_p = macaron_layer(
            yp, lambda h: mix_prompt(h, pos_p, lw, lam, lam_init), lw)
        ys, r_s, k_s, v_s = macaron_layer(
            ys, lambda h: mix_sample(h, pos_s, page_table, cache_mla, cache_diff_k, cache_diff_v,
                                     l, lw, lam, lam_init), lw)
        mla_p.append(r_p); dk_p.append(k_p); dv_p.append(v_p)
        mla_s.append(r_s); dk_s.append(k_s); dv_s.append(v_s)
    return (yp, ys, jnp.stack(mla_p), jnp.stack(dk_p), jnp.stack(dv_p),
            jnp.stack(mla_s), jnp.stack(dk_s), jnp.stack(dv_s))
```

```python
import functools
import math

import jax
import jax.numpy as jnp
from jax import lax
from jax.experimental import pallas as pl
from jax.experimental.pallas import tpu as pltpu

F32 = jnp.float32
BF16 = jnp.bfloat16

D_MODEL = 1024
MLA_HEADS = 8
MLA_V = 64
MLA_NOPE = 64
MLA_ROPE = 32
MLA_QK = MLA_NOPE + MLA_ROPE
Q_LORA = 256
KV_LORA = 128
MLA_ROW = KV_LORA + MLA_ROPE
DIFF_D = 64
DIFF_HD = 2 * DIFF_D
DIFF_HEADS = 4
DIFF_ROT = DIFF_D // 4
D_FF = 2816
ROPE_THETA = 500000.0
EPS = 1e-6
NEG = -1e30
LOG2E = 1.4426950408889634

LANE = 128
V7X_VMEM_BYTES = 64 * 1024 * 1024

TOKEN_TILE = 512
ATTN_TILE = 512
DECODE_PAGES = 8
NEW_PAD = 16

_C_CQ = 0
_C_CKV = _C_CQ + Q_LORA
_C_DQ = _C_CKV + KV_LORA
_C_DK = _C_DQ + DIFF_HEADS * DIFF_HD
_C_DV = _C_DK + DIFF_HEADS * DIFF_HD
_C_DQS = _C_DV + DIFF_HEADS * DIFF_HD
_C_DKS = _C_DQS + DIFF_HEADS * DIFF_HD
_C_KRA = _C_DKS + DIFF_HEADS * DIFF_HD
_C_KRB = _C_KRA + LANE
IN_COLS_P = _C_KRB + LANE
DIFF_W = DIFF_HEADS * DIFF_HD
MLA_W = MLA_HEADS * LANE


def _vmem_limit(nbytes):
    return int(min(max(nbytes, 32 * 1024 * 1024), V7X_VMEM_BYTES - 8 * 1024 * 1024))


def _rms(x, n):
    return x * lax.rsqrt(jnp.sum(x * x, axis=-1, keepdims=True) * (1.0 / n) + EPS)


def _dot(a, b):
    return jnp.dot(a, b, preferred_element_type=F32)


def _ffn_kernel(*refs, with_attn):
    if with_attn:
        x_ref, cm_ref, cd_ref, wom_ref, wod_ref, g_ref, wg_ref, wu_ref, wd_ref, o_ref = refs
        x = x_ref[...] + _dot(cm_ref[...], wom_ref[...]) + _dot(cd_ref[...], wod_ref[...])
    else:
        x_ref, g_ref, wg_ref, wu_ref, wd_ref, o_ref = refs
        x = x_ref[...]
    h = (_rms(x, D_MODEL) * g_ref[...]).astype(BF16)
    g = _dot(h, wg_ref[...])
    u = _dot(h, wu_ref[...])
    a = (g * (1.0 / (1.0 + jnp.exp(-g))) * u).astype(BF16)
    o_ref[...] = x + 0.5 * _dot(a, wd_ref[...])


def _const_spec(shape):
    return pl.BlockSpec(shape, lambda i: (0,) * len(shape), pipeline_mode=pl.Buffered(1))


def _ffn_call(x, norm_g, wg, wu, wd, attn=None):
    n = x.shape[0]
    tm = min(TOKEN_TILE, n)
    assert n % tm == 0
    row = lambda w: pl.BlockSpec((tm, w), lambda i: (i, 0))
    in_specs = [row(D_MODEL)]
    args = [x]
    if attn is not None:
        cm, cd, wom, wod = attn
        in_specs += [row(MLA_W), row(DIFF_W), _const_spec(wom.shape), _const_spec(wod.shape)]
        args += [cm, cd, wom, wod]
    in_specs += [_const_spec((1, D_MODEL)), _const_spec(wg.shape), _const_spec(wu.shape), _const_spec(wd.shape)]
    args += [norm_g.reshape(1, D_MODEL), wg, wu, wd]
    weights = 3 * D_MODEL * D_FF * 2 + (MLA_W + DIFF_W) * D_MODEL * 2
    tiles = tm * (4 * D_MODEL * 4 + 2 * D_FF * 4 + D_FF * 2 + 3 * D_MODEL * 4 + 2 * (MLA_W + DIFF_W) * 2)
    return pl.pallas_call(
        functools.partial(_ffn_kernel, with_attn=attn is not None),
        grid=(n // tm,),
        in_specs=in_specs,
        out_specs=row(D_MODEL),
        out_shape=jax.ShapeDtypeStruct((n, D_MODEL), F32),
        compiler_params=pltpu.CompilerParams(
            dimension_semantics=("arbitrary",), vmem_limit_bytes=_vmem_limit(weights + tiles)),
        name="ffn_attn" if attn is not None else "ffn",
    )(*args)


def _proj_kernel(*refs, sample):
    if sample:
        (x_ref, tab_ref, vec_ref, win_ref, wq_ref, b64_ref, wabs_ref,
         row_ref, dk_ref, dv_ref, qa_ref, dq_ref) = refs
    else:
        (x_ref, tab_ref, vec_ref, win_ref, wq_ref, b64_ref, wkv_ref,
         row_ref, dk_ref, dv_ref, qm_ref, km_ref, vm_ref, qd_ref, kd_ref, vd_ref) = refs
    x = x_ref[...]
    h = (_rms(x, D_MODEL) * vec_ref[0:1, :]).astype(BF16)
    z = _dot(h, win_ref[...])
    tab = tab_ref[...]
    cq_t, sq_t, ck_t, sk_t, cd_t, sd_t = (tab[:, i * LANE:(i + 1) * LANE] for i in range(6))
    gq = vec_ref[1:2, 384:512]
    gk = vec_ref[1:2, 512:640]

    ckv = _rms(z[:, _C_CKV:_C_CKV + KV_LORA], KV_LORA) * vec_ref[1:2, 256:384]
    krt = z[:, _C_KRA:_C_KRA + LANE] * ck_t + z[:, _C_KRB:_C_KRB + LANE] * sk_t
    row256 = jnp.concatenate([ckv, krt], axis=-1)
    row_ref[...] = row256[:, :MLA_ROW]

    c512 = jnp.concatenate([cd_t] * (DIFF_W // LANE), axis=-1)
    s512 = jnp.concatenate([sd_t] * (DIFF_W // LANE), axis=-1)

    def diff_norm(a, a_sw, g, g_sw):
        ss = _dot((a * a).astype(BF16), b64_ref[...])
        return lax.rsqrt(ss * (1.0 / DIFF_D) + EPS) * (a * g * c512 + a_sw * g_sw * s512)

    dq = diff_norm(z[:, _C_DQ:_C_DQ + DIFF_W], z[:, _C_DQS:_C_DQS + DIFF_W],
                   vec_ref[2:3, 0:DIFF_W], vec_ref[2:3, DIFF_W:2 * DIFF_W])
    dk = diff_norm(z[:, _C_DK:_C_DK + DIFF_W], z[:, _C_DKS:_C_DKS + DIFF_W],
                   vec_ref[3:4, 0:DIFF_W], vec_ref[3:4, DIFF_W:2 * DIFF_W])
    dv = z[:, _C_DV:_C_DV + DIFF_W]
    dk_ref[...] = dk
    dv_ref[...] = dv
    dq = dq * (DIFF_D ** -0.5 * LOG2E)

    cqn = (_rms(z[:, _C_CQ:_C_CQ + Q_LORA], Q_LORA) * vec_ref[1:2, 0:Q_LORA]).astype(BF16)
    q2 = _dot(cqn, wq_ref[...])
    if not sample:
        kv = _dot(row256.astype(BF16), wkv_ref[...])
    for hh in range(MLA_HEADS):
        sl = slice(hh * LANE, (hh + 1) * LANE)
        qh = q2[:, sl] * cq_t + q2[:, MLA_W + hh * LANE:MLA_W + (hh + 1) * LANE] * sq_t
        qn = _rms(qh, MLA_QK) * gq * (MLA_QK ** -0.5 * LOG2E)
        if sample:
            qa_ref[hh] = _dot((qn * gk).astype(BF16), wabs_ref[hh])
        else:
            qm_ref[hh] = qn.astype(BF16)
            km_ref[hh] = (_rms(kv[:, sl], MLA_QK) * gk).astype(BF16)
            vm_ref[hh] = kv[:, MLA_W + hh * LANE:MLA_W + (hh + 1) * LANE].astype(BF16)
    if sample:
        dq_ref[...] = dq
    else:
        lane = lax.broadcasted_iota(jnp.int32, (x.shape[0], LANE), 1)
        for hh in range(DIFF_HEADS):
            sl = slice(hh * LANE, (hh + 1) * LANE)
            qd_ref[hh, 0] = jnp.where(lane < DIFF_D, dq[:, sl], 0.0).astype(BF16)
            qd_ref[hh, 1] = jnp.where(lane >= DIFF_D, dq[:, sl], 0.0).astype(BF16)
            kd_ref[hh] = dk[:, sl].astype(BF16)
            vd_ref[hh] = dv[:, sl].astype(BF16)


def _proj_call(x, tab, lp, sample):
    n = x.shape[0]
    tm = min(TOKEN_TILE, n)
    assert n % tm == 0 and tab.shape[0] % tm == 0
    tab_blocks = tab.shape[0] // tm
    row = lambda w: pl.BlockSpec((tm, w), lambda i: (i, 0))
    const = lambda a: pl.BlockSpec(a.shape, lambda i: (0,) * a.ndim)
    heads = lambda nh: pl.BlockSpec((nh, tm, LANE), lambda i: (0, i, 0))
    in_specs = [row(D_MODEL), pl.BlockSpec((tm, 6 * LANE), lambda i: (i % tab_blocks, 0)),
                const(lp["vec"]), const(lp["w_in"]), const(lp["wq"]), const(lp["b64"])]
    args = [x, tab, lp["vec"], lp["w_in"], lp["wq"], lp["b64"]]
    out_specs = [row(MLA_ROW), row(DIFF_W), row(DIFF_W)]
    out_shape = [jax.ShapeDtypeStruct((n, MLA_ROW), F32), jax.ShapeDtypeStruct((n, DIFF_W), F32),
                 jax.ShapeDtypeStruct((n, DIFF_W), F32)]
    if sample:
        in_specs.append(const(lp["wabs"]))
        args.append(lp["wabs"])
        out_specs += [pl.BlockSpec((MLA_HEADS, tm, 2 * LANE), lambda i: (0, i, 0)), row(DIFF_W)]
        out_shape += [jax.ShapeDtypeStruct((MLA_HEADS, n, 2 * LANE), F32), jax.ShapeDtypeStruct((n, DIFF_W), F32)]
    else:
        in_specs.append(const(lp["wkv"]))
        args.append(lp["wkv"])
        out_specs += [heads(MLA_HEADS)] * 3 + [pl.BlockSpec((DIFF_HEADS, 2, tm, LANE), lambda i: (0, 0, i, 0))]
        out_specs += [heads(DIFF_HEADS)] * 2
        out_shape += [jax.ShapeDtypeStruct((MLA_HEADS, n, LANE), BF16)] * 3
        out_shape += [jax.ShapeDtypeStruct((DIFF_HEADS, 2, n, LANE), BF16)]
        out_shape += [jax.ShapeDtypeStruct((DIFF_HEADS, n, LANE), BF16)] * 2
    weights = 2 * 2 * (D_MODEL * IN_COLS_P + 2 * Q_LORA * 2 * MLA_W + DIFF_W * DIFF_W)
    tiles = tm * (IN_COLS_P * 4 * 2 + 4 * MLA_W * 4 + 8 * DIFF_W * 4 + 2 * (D_MODEL + 6 * LANE) * 4
                  + 2 * (MLA_ROW + 2 * DIFF_W) * 4 + 2 * (3 * MLA_W + 4 * DIFF_W) * 4)
    return pl.pallas_call(
        functools.partial(_proj_kernel, sample=sample),
        grid=(n // tm,),
        in_specs=in_specs,
        out_specs=out_specs,
        out_shape=out_shape,
        compiler_params=pltpu.CompilerParams(
            dimension_semantics=("arbitrary",), vmem_limit_bytes=_vmem_limit(weights + tiles)),
        name="proj_sample" if sample else "proj_prompt",
    )(*args)


def _diff_finish(o1, o2, par):
    d = o1 - par[0:1, :] * o2
    return _rms(d, DIFF_HD) * par[1:2, :]


def _flash_kernel(*refs, nmaps, blk):
    if nmaps == 2:
        q_ref, k_ref, v_ref, par_ref, o_ref = refs
    else:
        q_ref, k_ref, v_ref, o_ref = refs
    i = pl.program_id(2)
    m_rows = nmaps * blk
    q = q_ref[...].reshape(m_rows, LANE)

    def step(j, carry, diagonal):
        m, l, acc = carry
        start = pl.multiple_of(j * blk, blk)
        k = k_ref[pl.ds(start, blk), :]
        v = v_ref[pl.ds(start, blk), :]
        s = lax.dot_general(q, k, (((1,), (1,)), ((), ())), preferred_element_type=F32)
        if diagonal:
            r = lax.broadcasted_iota(jnp.int32, (m_rows, blk), 0)
            c = lax.broadcasted_iota(jnp.int32, (m_rows, blk), 1)
            if nmaps == 2:
                r = jnp.where(r >= blk, r - blk, r)
            s = jnp.where(c <= r, s, NEG)
        m_new = jnp.maximum(m, jnp.max(s, axis=-1, keepdims=True))
        p = jnp.exp2(s - m_new)
        corr = jnp.exp2(m - m_new)
        l = l * corr + jnp.sum(p, axis=-1, keepdims=True)
        acc = acc * corr + _dot(p.astype(BF16), v)
        return m_new, l, acc

    init = (jnp.full((m_rows, 1), NEG, F32), jnp.zeros((m_rows, 1), F32), jnp.zeros((m_rows, LANE), F32))
    carry = lax.fori_loop(0, i, lambda j, c: step(j, c, False), init)
    _, l, acc = step(i, carry, True)
    o = acc / l
    if nmaps == 2:
        o_ref[...] = _diff_finish(o[:blk], o[blk:], par_ref[...]).astype(o_ref.dtype)
    else:
        o_ref[...] = o.astype(o_ref.dtype)


def _flash_call(q, k, v, batch, seq, par=None):
    nmaps = 2 if q.ndim == 4 else 1
    units = q.shape[0]
    blk = min(ATTN_TILE, seq)
    assert seq % blk == 0
    nq = seq // blk
    if nmaps == 2:
        q_spec = pl.BlockSpec((None, 2, blk, LANE), lambda b, u, i: (u, 0, b * nq + i, 0))
    else:
        q_spec = pl.BlockSpec((None, blk, LANE), lambda b, u, i: (u, b * nq + i, 0))
    kv_spec = pl.BlockSpec((None, seq, LANE), lambda b, u, i: (u, b, 0))
    in_specs = [q_spec, kv_spec, kv_spec]
    args = [q, k, v]
    if nmaps == 2:
        in_specs.append(pl.BlockSpec(par.shape, lambda b, u, i: (0, 0)))
        args.append(par)
    need = 2 * 2 * seq * LANE * 2 + nmaps * blk * (6 * blk * 4 + 8 * LANE * 4)
    return pl.pallas_call(
        functools.partial(_flash_kernel, nmaps=nmaps, blk=blk),
        grid=(batch, units, nq),
        in_specs=in_specs,
        out_specs=pl.BlockSpec((blk, LANE), lambda b, u, i: (b * nq + i, u)),
        out_shape=jax.ShapeDtypeStruct((batch * seq, units * LANE), BF16),
        compiler_params=pltpu.CompilerParams(
            dimension_semantics=("arbitrary", "arbitrary", "arbitrary"), vmem_limit_bytes=_vmem_limit(need)),
        name="flash_diff" if nmaps == 2 else "flash_mla",
    )(*args)


def _col_of(row_vec):
    n = row_vec.shape[1]
    eye = lax.broadcasted_iota(jnp.int32, (n, n), 0) == lax.broadcasted_iota(jnp.int32, (n, n), 1)
    return jnp.sum(jnp.where(eye, row_vec, 0.0), axis=1, keepdims=True)


def _decode_kernel(pt_ref, cm_hbm, ck_hbm, cv_hbm, qa_ref, qd_ref, rown_ref, kn_ref, vn_ref, wuk_ref,
                   om_ref, od_ref,
                   mbuf, kbuf, vbuf, sems, mm_ref, lm_ref, am_ref, md_ref, ld_ref, ad_ref,
                   *, page, pages_per_step, page_base, sample_len):
    b = pl.program_id(0)
    c = pl.program_id(1)
    nchunk = pl.num_programs(1)
    step = b * nchunk + c
    slot = step % 2
    tok = page * pages_per_step
    ncols = MLA_HEADS * sample_len

    def copies(bb, cc, sl):
        out = []
        for j in range(pages_per_step):
            pg = pt_ref[bb, cc * pages_per_step + j] + page_base
            out.append(pltpu.make_async_copy(
                cm_hbm.at[pl.ds(pl.multiple_of(pg * page, page), page)],
                mbuf.at[sl, pl.ds(j * page, page)], sems.at[sl, 0]))
            rows = pl.ds(pl.multiple_of(pg * (page * DIFF_HEADS), page * DIFF_HEADS), page * DIFF_HEADS)
            dst = pl.ds(j * page * DIFF_HEADS, page * DIFF_HEADS)
            out.append(pltpu.make_async_copy(ck_hbm.at[rows], kbuf.at[sl, dst], sems.at[sl, 1]))
            out.append(pltpu.make_async_copy(cv_hbm.at[rows], vbuf.at[sl, dst], sems.at[sl, 2]))
        return out

    @pl.when(step == 0)
    def _():
        for cp in copies(0, 0, 0):
            cp.start()

    @pl.when(step + 1 < pl.num_programs(0) * nchunk)
    def _():
        nxt = step + 1
        for cp in copies(nxt // nchunk, nxt % nchunk, 1 - slot):
            cp.start()

    @pl.when(c == 0)
    def _():
        mm_ref[...] = jnp.full(mm_ref.shape, NEG, F32)
        md_ref[...] = jnp.full(md_ref.shape, NEG, F32)
        lm_ref[...] = jnp.zeros(lm_ref.shape, F32)
        ld_ref[...] = jnp.zeros(ld_ref.shape, F32)
        am_ref[...] = jnp.zeros(am_ref.shape, F32)
        ad_ref[...] = jnp.zeros(ad_ref.shape, F32)

    for cp in copies(b, c, slot):
        cp.wait()

    def online(s_t, vals, m_ref, l_ref, a_ref):
        m_old = m_ref[...]
        m_new = jnp.maximum(m_old, jnp.max(s_t, axis=0, keepdims=True))
        p = jnp.exp2(s_t - m_new)
        corr = jnp.exp2(m_old - m_new)
        l_ref[...] = l_ref[...] * corr + jnp.sum(p, axis=0, keepdims=True)
        pv = lax.dot_general(p.astype(BF16), vals, (((0,), (0,)), ((), ())), preferred_element_type=F32)
        a_ref[...] = a_ref[...] * _col_of(corr) + pv
        m_ref[...] = m_new

    def mla_scores(rowf):
        rowb = rowf.astype(BF16)
        kn = _dot(rowb, wuk_ref[...])
        ksq = kn * kn
        s4 = ksq[:, 0:LANE]
        for t in range(1, MLA_HEADS * MLA_NOPE // LANE):
            s4 = s4 + ksq[:, t * LANE:(t + 1) * LANE]
        shift = MLA_HEADS
        while shift < LANE:
            s4 = s4 + pltpu.roll(s4, shift, axis=1)
            shift *= 2
        kr = rowf[:, KV_LORA:MLA_ROW]
        ss = s4[:, :ncols] + jnp.sum(kr * kr, axis=-1, keepdims=True)
        rs = lax.rsqrt(ss * (1.0 / MLA_QK) + EPS)
        return _dot(rowb, qa_ref[0]) * rs, rowb

    def diff_operands(kref, vref):
        kc = jnp.concatenate([kref(hh) for hh in range(DIFF_HEADS)], axis=-1).astype(BF16)
        vc = jnp.concatenate([vref(hh) for hh in range(DIFF_HEADS)], axis=-1).astype(BF16)
        return _dot(kc, qd_ref[0]), vc

    s_m, rowb = mla_scores(mbuf[slot])
    online(s_m, rowb[:, :KV_LORA], mm_ref, lm_ref, am_ref)
    s_d, vc = diff_operands(lambda hh: kbuf[slot, pl.ds(hh, tok, stride=DIFF_HEADS), :],
                            lambda hh: vbuf[slot, pl.ds(hh, tok, stride=DIFF_HEADS), :])
    online(s_d, vc, md_ref, ld_ref, ad_ref)

    @pl.when(c == nchunk - 1)
    def _():
        trow = lax.broadcasted_iota(jnp.int32, (NEW_PAD, ncols), 0)
        col = lax.broadcasted_iota(jnp.int32, (NEW_PAD, ncols), 1)
        sn_m, rownb = mla_scores(rown_ref[0])
        online(jnp.where(trow <= col // MLA_HEADS, sn_m, NEG), rownb[:, :KV_LORA], mm_ref, lm_ref, am_ref)
        sn_d, vnc = diff_operands(lambda hh: kn_ref[0][:, hh * LANE:(hh + 1) * LANE],
                                  lambda hh: vn_ref[0][:, hh * LANE:(hh + 1) * LANE])
        online(jnp.where(trow <= col % sample_len, sn_d, NEG), vnc, md_ref, ld_ref, ad_ref)
        om_ref[0] = am_ref[...] / _col_of(lm_ref[...])
        l_d = _col_of(ld_ref[...])
        rows_per_head = 2 * sample_len
        for hh in range(DIFF_HEADS):
            rsl = slice(hh * rows_per_head, (hh + 1) * rows_per_head)
            od_ref[0, rsl, :] = ad_ref[rsl, hh * LANE:(hh + 1) * LANE] / l_d[rsl]


def _decode_call(page_table, cm_rows, ck_rows, cv_rows, qa, qd, row_new, k_new, v_new, wuk, page, page_base):
    bsz, n_pages = page_table.shape
    pps = min(DECODE_PAGES, n_pages)
    assert n_pages % pps == 0
    nchunk = n_pages // pps
    ncols = qa.shape[2]
    sample_len = ncols // MLA_HEADS
    tok = page * pps
    per_b = lambda shape: pl.BlockSpec((1,) + shape, lambda b, c, pt: (b, 0, 0))
    grid_spec = pltpu.PrefetchScalarGridSpec(
        num_scalar_prefetch=1,
        grid=(bsz, nchunk),
        in_specs=[pl.BlockSpec(memory_space=pl.ANY)] * 3 + [
            per_b((MLA_ROW, ncols)), per_b((DIFF_W, ncols)),
            per_b((NEW_PAD, MLA_ROW)), per_b((NEW_PAD, DIFF_W)), per_b((NEW_PAD, DIFF_W)),
            pl.BlockSpec(wuk.shape, lambda b, c, pt: (0, 0))],
        out_specs=[per_b((ncols, LANE)), per_b((ncols, LANE))],
        scratch_shapes=[
            pltpu.VMEM((2, tok, MLA_ROW), F32),
            pltpu.VMEM((2, tok * DIFF_HEADS, LANE), F32),
            pltpu.VMEM((2, tok * DIFF_HEADS, LANE), F32),
            pltpu.SemaphoreType.DMA((2, 3)),
            pltpu.VMEM((1, ncols), F32), pltpu.VMEM((1, ncols), F32), pltpu.VMEM((ncols, KV_LORA), F32),
            pltpu.VMEM((1, ncols), F32), pltpu.VMEM((1, ncols), F32), pltpu.VMEM((ncols, DIFF_W), F32),
        ])
    need = 2 * tok * (2 * LANE + 2 * DIFF_W) * 4 + tok * (6 * DIFF_W * 4 + 8 * LANE * 4)
    return pl.pallas_call(
        functools.partial(_decode_kernel, page=page, pages_per_step=pps, page_base=page_base,
                          sample_len=sample_len),
        grid_spec=grid_spec,
        out_shape=[jax.ShapeDtypeStruct((bsz, ncols, LANE), F32)] * 2,
        compiler_params=pltpu.CompilerParams(
            dimension_semantics=("arbitrary", "arbitrary"), vmem_limit_bytes=_vmem_limit(need)),
        name="decode",
    )(page_table, cm_rows, ck_rows, cv_rows, qa, qd, row_new, k_new, v_new, wuk)


def _sample_combine_kernel(om_ref, od_ref, wuv_ref, par_ref, cm_ref, cd_ref):
    for hh in range(MLA_HEADS):
        cm_ref[:, hh * LANE:(hh + 1) * LANE] = _dot(om_ref[hh].astype(BF16), wuv_ref[hh]).astype(BF16)
    for hh in range(DIFF_HEADS):
        cd_ref[:, hh * LANE:(hh + 1) * LANE] = _diff_finish(od_ref[hh, 0], od_ref[hh, 1], par_ref[...]).astype(BF16)


def _sample_combine_call(om, od, wuv, par):
    n = om.shape[1]
    full = lambda a: pl.BlockSpec(a.shape, lambda i: (0,) * a.ndim)
    return pl.pallas_call(
        _sample_combine_kernel,
        grid=(1,),
        in_specs=[full(om), full(od), full(wuv), full(par)],
        out_specs=[pl.BlockSpec((n, MLA_W), lambda i: (0, 0)), pl.BlockSpec((n, DIFF_W), lambda i: (0, 0))],
        out_shape=[jax.ShapeDtypeStruct((n, MLA_W), BF16), jax.ShapeDtypeStruct((n, DIFF_W), BF16)],
        name="sample_combine",
    )(om, od, wuv, par)


def _rot_half_cols(w, group, half):
    w3 = w.reshape(w.shape[0], -1, group)
    zeros = jnp.zeros_like(w3[..., 2 * half:])
    return jnp.concatenate([-w3[..., half:2 * half], w3[..., :half], zeros], axis=-1).reshape(w.shape)


def _rot_half_gain(g, half):
    return jnp.concatenate([g[half:2 * half], g[:half], jnp.zeros_like(g[2 * half:])])


def _prep_layer(w_in, mix_norm, mla_q_norm, mla_kv_norm, w_uq, w_uk, w_uv, mla_q_gain, mla_k_gain,
                diff_q_gain, diff_k_gain):
    o1 = Q_LORA
    o2 = o1 + KV_LORA
    o3 = o2 + MLA_ROPE
    o4 = o3 + DIFF_W
    o5 = o4 + DIFF_W
    wkr = w_in[:, o2:o3]
    wdq, wdk, wdv = w_in[:, o3:o4], w_in[:, o4:o5], w_in[:, o5:]
    pad = jnp.zeros((D_MODEL, LANE - MLA_ROPE), F32)
    w_in_p = jnp.concatenate(
        [w_in[:, :o1], w_in[:, o1:o2], wdq, wdk, wdv,
         _rot_half_cols(wdq, DIFF_D, DIFF_ROT // 2), _rot_half_cols(wdk, DIFF_D, DIFF_ROT // 2),
         wkr, pad, _rot_half_cols(wkr, MLA_ROPE, MLA_ROPE // 2), pad], axis=1).astype(BF16)
    assert w_in_p.shape[1] == IN_COLS_P

    hpad = jnp.zeros((Q_LORA, MLA_HEADS, LANE - MLA_QK), F32)
    wq = jnp.concatenate([w_uq, hpad], axis=-1).reshape(Q_LORA, MLA_W)
    wq_rope = w_uq[..., MLA_NOPE:]
    wq_sw = jnp.concatenate(
        [jnp.zeros((Q_LORA, MLA_HEADS, MLA_NOPE), F32),
         _rot_half_cols(wq_rope.reshape(Q_LORA, -1), MLA_ROPE, MLA_ROPE // 2).reshape(Q_LORA, MLA_HEADS, MLA_ROPE),
         hpad], axis=-1).reshape(Q_LORA, MLA_W)
    wq2 = jnp.concatenate([wq, wq_sw], axis=1).astype(BF16)

    vpad = jnp.zeros((KV_LORA, MLA_HEADS, LANE - MLA_NOPE), F32)
    wk_c = jnp.concatenate([w_uk, vpad], axis=-1).reshape(KV_LORA, MLA_W)
    place = jnp.concatenate([jnp.zeros((MLA_ROPE, MLA_NOPE), F32), jnp.eye(MLA_ROPE, dtype=F32),
                             jnp.zeros((MLA_ROPE, LANE - MLA_QK), F32)], axis=1)
    wk_r = jnp.tile(place, (1, MLA_HEADS))
    wk = jnp.concatenate([wk_c, wk_r, jnp.zeros((LANE - MLA_ROPE, MLA_W), F32)], axis=0)
    wv = jnp.concatenate([jnp.concatenate([w_uv, vpad], axis=-1).reshape(KV_LORA, MLA_W),
                          jnp.zeros((LANE, MLA_W), F32)], axis=0)
    wkv = jnp.concatenate([wk, wv], axis=1).astype(BF16)

    wabs_top = jnp.concatenate([jnp.transpose(w_uk, (1, 2, 0)), jnp.zeros((MLA_HEADS, MLA_NOPE, LANE), F32)], axis=-1)
    wabs_mid = jnp.concatenate([jnp.zeros((MLA_ROPE, LANE), F32), jnp.eye(MLA_ROPE, dtype=F32),
                                jnp.zeros((MLA_ROPE, LANE - MLA_ROPE), F32)], axis=1)
    wabs = jnp.concatenate([wabs_top, jnp.broadcast_to(wabs_mid, (MLA_HEADS,) + wabs_mid.shape),
                            jnp.zeros((MLA_HEADS, LANE - MLA_QK, 2 * LANE), F32)], axis=1).astype(BF16)

    wuk_perm = jnp.concatenate([jnp.transpose(w_uk, (0, 2, 1)).reshape(KV_LORA, MLA_HEADS * MLA_NOPE),
                                jnp.zeros((MLA_ROPE, MLA_HEADS * MLA_NOPE), F32)], axis=0).astype(BF16)
    wuv_pad = jnp.concatenate([jnp.transpose(w_uv, (1, 0, 2)), jnp.zeros((MLA_HEADS, KV_LORA, LANE - MLA_V), F32)],
                              axis=-1).astype(BF16)

    qpad = jnp.zeros((LANE - MLA_QK,), F32)
    row1 = jnp.concatenate([mla_q_norm, mla_kv_norm, mla_q_gain, qpad, mla_k_gain, qpad,
                            jnp.zeros((D_MODEL - Q_LORA - KV_LORA - 2 * LANE,), F32)])
    reps = DIFF_W // DIFF_D
    row2 = jnp.concatenate([jnp.tile(diff_q_gain, reps), jnp.tile(_rot_half_gain(diff_q_gain, DIFF_ROT // 2), reps)])
    row3 = jnp.concatenate([jnp.tile(diff_k_gain, reps), jnp.tile(_rot_half_gain(diff_k_gain, DIFF_ROT // 2), reps)])
    vec = jnp.stack([mix_norm, row1, row2, row3] + [jnp.zeros((D_MODEL,), F32)] * 4)
    b64 = jnp.kron(jnp.eye(DIFF_W // DIFF_D, dtype=F32), jnp.ones((DIFF_D, DIFF_D), F32)).astype(BF16)
    return dict(w_in=w_in_p, wq=wq2, wkv=wkv, wabs=wabs, wuk_perm=wuk_perm, wuv_pad=wuv_pad, vec=vec, b64=b64)


def _rope_tables(pos):
    n = pos.shape[0]

    def cs(r):
        half = r // 2
        inv = jnp.power(jnp.float32(ROPE_THETA), -jnp.arange(half, dtype=F32) * (2.0 / r))
        ang = pos.astype(F32)[:, None] * inv[None, :]
        return jnp.cos(ang), jnp.sin(ang)

    cm, sm = cs(MLA_ROPE)
    cd, sd = cs(DIFF_ROT)
    ones = lambda w: jnp.ones((n, w), F32)
    zeros = lambda w: jnp.zeros((n, w), F32)
    c64 = jnp.concatenate([cd, cd, ones(DIFF_D - DIFF_ROT)], axis=1)
    s64 = jnp.concatenate([sd, sd, zeros(DIFF_D - DIFF_ROT)], axis=1)
    return jnp.concatenate([
        ones(MLA_NOPE), cm, cm, zeros(LANE - MLA_QK),
        zeros(MLA_NOPE), sm, sm, zeros(LANE - MLA_QK),
        cm, cm, zeros(LANE - MLA_ROPE),
        sm, sm, zeros(LANE - MLA_ROPE),
        c64, c64, s64, s64], axis=1)


def kernel(x_prompt, x_sample, cache_mla, cache_diff_k, cache_diff_v, page_table, ffn1_norm, ffn1_w_gate, ffn1_w_up, ffn1_w_down, mix_norm, w_in, mla_q_norm, mla_kv_norm, w_uq, w_uk, w_uv, mla_q_gain, mla_k_gain, diff_q_gain, diff_k_gain, diff_lq1, diff_lk1, diff_lq2, diff_lk2, diff_sub_gain, w_o, ffn2_norm, ffn2_w_gate, ffn2_w_up, ffn2_w_down):
    depth = w_in.shape[0]
    bp, seq, _ = x_prompt.shape
    bs, ts, _ = x_sample.shape
    n_pool, page = cache_mla.shape[1], cache_mla.shape[2]
    past_len = page_table.shape[1] * page
    assert ts <= NEW_PAD and MLA_HEADS * ts == 2 * DIFF_HEADS * ts

    cm_rows = cache_mla.reshape(-1, MLA_ROW)
    ck_rows = cache_diff_k.reshape(-1, DIFF_HD)
    cv_rows = cache_diff_v.reshape(-1, DIFF_HD)

    tab_p = _rope_tables(jnp.arange(seq, dtype=jnp.int32))
    tab_s = jnp.tile(_rope_tables(past_len + jnp.arange(ts, dtype=jnp.int32)), (bs, 1))

    yp = x_prompt.reshape(bp * seq, D_MODEL)
    ys = x_sample.reshape(bs * ts, D_MODEL)
    outs = [[] for _ in range(6)]
    for l in range(depth):
        lp = _prep_layer(w_in[l], mix_norm[l], mla_q_norm[l], mla_kv_norm[l], w_uq[l], w_uk[l], w_uv[l],
                         mla_q_gain[l], mla_k_gain[l], diff_q_gain[l], diff_k_gain[l])
        lam_init = 0.8 - 0.6 * math.exp(-0.3 * l)
        lam = (jnp.exp(jnp.sum(diff_lq1[l] * diff_lk1[l])) - jnp.exp(jnp.sum(diff_lq2[l] * diff_lk2[l])) + lam_init)
        par = jnp.stack([jnp.broadcast_to(lam, (DIFF_HD,)), diff_sub_gain[l] * (1.0 - lam_init)]
                        + [jnp.zeros((DIFF_HD,), F32)] * 6)
        f1 = (ffn1_w_gate[l].astype(BF16), ffn1_w_up[l].astype(BF16), ffn1_w_down[l].astype(BF16))
        f2 = (ffn2_w_gate[l].astype(BF16), ffn2_w_up[l].astype(BF16), ffn2_w_down[l].astype(BF16))
        wo_m = jnp.concatenate([w_o[l, :MLA_HEADS * MLA_V].reshape(MLA_HEADS, MLA_V, D_MODEL),
                                jnp.zeros((MLA_HEADS, LANE - MLA_V, D_MODEL), F32)], axis=1
                               ).reshape(MLA_W, D_MODEL).astype(BF16)
        wo_d = w_o[l, MLA_HEADS * MLA_V:].astype(BF16)

        x1p = _ffn_call(yp, ffn1_norm[l], *f1)
        row_p, dk_p, dv_p, qm, km, vm, qd, kd, vd = _proj_call(x1p, tab_p, lp, sample=False)
        cat_m = _flash_call(qm, km, vm, bp, seq)
        cat_d = _flash_call(qd, kd, vd, bp, seq, par=par)
        yp = _ffn_call(x1p, ffn2_norm[l], *f2, attn=(cat_m, cat_d, wo_m, wo_d))

        x1s = _ffn_call(ys, ffn1_norm[l], *f1)
        row_s, dk_s, dv_s, qabs, dq_s = _proj_call(x1s, tab_s, lp, sample=True)
        qa = jnp.transpose(qabs[:, :, :MLA_ROW].reshape(MLA_HEADS, bs, ts, MLA_ROW), (1, 3, 2, 0))
        qa = qa.reshape(bs, MLA_ROW, ts * MLA_HEADS).astype(BF16)
        dq5 = dq_s.reshape(bs, ts, DIFF_HEADS, 2, DIFF_D)
        qbd = jnp.einsum("bthmd,hH,mM->bhmdHMt", dq5, jnp.eye(DIFF_HEADS, dtype=F32), jnp.eye(2, dtype=F32))
        qbd = qbd.reshape(bs, DIFF_W, DIFF_HEADS * 2 * ts).astype(BF16)
        padn = lambda a: jnp.pad(a.reshape(bs, ts, -1), ((0, 0), (0, NEW_PAD - ts), (0, 0)))
        om, od = _decode_call(page_table, cm_rows, ck_rows, cv_rows, qa, qbd, padn(row_s), padn(dk_s), padn(dv_s),
                              lp["wuk_perm"], page, l * n_pool)
        om = jnp.transpose(om.reshape(bs, ts, MLA_HEADS, KV_LORA), (2, 0, 1, 3)).reshape(MLA_HEADS, bs * ts, KV_LORA)
        od = jnp.transpose(od.reshape(bs, DIFF_HEADS, 2, ts, DIFF_HD), (1, 2, 0, 3, 4)
                           ).reshape(DIFF_HEADS, 2, bs * ts, DIFF_HD)
        cat_ms, cat_ds = _sample_combine_call(om, od, lp["wuv_pad"], par)
        ys = _ffn_call(x1s, ffn2_norm[l], *f2, attn=(cat_ms, cat_ds, wo_m, wo_d))

        outs[0].append(row_p.reshape(bp, seq, MLA_ROW))
        outs[1].append(dk_p.reshape(bp, seq, DIFF_HEADS, DIFF_HD))
        outs[2].append(dv_p.reshape(bp, seq, DIFF_HEADS, DIFF_HD))
        outs[3].append(row_s.reshape(bs, ts, MLA_ROW))
        outs[4].append(dk_s.reshape(bs, ts, DIFF_HEADS, DIFF_HD))
        outs[5].append(dv_s.reshape(bs, ts, DIFF_HEADS, DIFF_HD))
    return (yp.reshape(bp, seq, D_MODEL), ys.reshape(bs, ts, D_MODEL)) + tuple(jnp.stack(o) for o in outs)
```

```python
import functools
import math

import jax
import jax.numpy as jnp
from jax import lax
from jax.experimental import pallas as pl
from jax.experimental.pallas import tpu as pltpu

F32 = jnp.float32
BF16 = jnp.bfloat16

D_MODEL = 1024
MLA_HEADS = 8
MLA_V = 64
MLA_NOPE = 64
MLA_ROPE = 32
MLA_QK = MLA_NOPE + MLA_ROPE
Q_LORA = 256
KV_LORA = 128
MLA_ROW = KV_LORA + MLA_ROPE
DIFF_D = 64
DIFF_HD = 2 * DIFF_D
DIFF_HEADS = 4
DIFF_ROT = DIFF_D // 4
D_FF = 2816
ROPE_THETA = 500000.0
EPS = 1e-6
NEG = -1e30
LOG2E = 1.4426950408889634

LANE = 128
V7X_VMEM_BYTES = 64 * 1024 * 1024

TOKEN_TILE = 512
ATTN_TILE = 1024
MLA_HEADS_PER_STEP = 2
DECODE_PAGES = 16
NEW_PAD = 16

_C_CQ = 0
_C_CKV = _C_CQ + Q_LORA
_C_DQ = _C_CKV + KV_LORA
_C_DK = _C_DQ + DIFF_HEADS * DIFF_HD
_C_DV = _C_DK + DIFF_HEADS * DIFF_HD
_C_DQS = _C_DV + DIFF_HEADS * DIFF_HD
_C_DKS = _C_DQS + DIFF_HEADS * DIFF_HD
_C_KRA = _C_DKS + DIFF_HEADS * DIFF_HD
_C_KRB = _C_KRA + LANE
IN_COLS_P = _C_KRB + LANE
DIFF_W = DIFF_HEADS * DIFF_HD
MLA_W = MLA_HEADS * LANE


def _vmem_limit(nbytes):
    return int(min(max(nbytes, 32 * 1024 * 1024), V7X_VMEM_BYTES - 8 * 1024 * 1024))


def _rms(x, n):
    return x * lax.rsqrt(jnp.sum(x * x, axis=-1, keepdims=True) * (1.0 / n) + EPS)


def _dot(a, b):
    return jnp.dot(a, b, preferred_element_type=F32)


def _dot_nt(a, b):
    return lax.dot_general(a, b, (((1,), (1,)), ((), ())), preferred_element_type=F32)


def _ffn_kernel(*refs, with_attn):
    if with_attn:
        x_ref, cm_ref, cd_ref, wom_ref, wod_ref, g_ref, wg_ref, wu_ref, wd_ref, o_ref = refs
        x = x_ref[...] + _dot(cm_ref[...], wom_ref[...]) + _dot(cd_ref[...], wod_ref[...])
    else:
        x_ref, g_ref, wg_ref, wu_ref, wd_ref, o_ref = refs
        x = x_ref[...]
    h = (_rms(x, D_MODEL) * g_ref[...]).astype(BF16)
    g = _dot(h, wg_ref[...])
    u = _dot(h, wu_ref[...])
    a = (g * (1.0 / (1.0 + jnp.exp(-g))) * u).astype(BF16)
    o_ref[...] = x + 0.5 * _dot(a, wd_ref[...])


def _const_spec(shape):
    return pl.BlockSpec(shape, lambda i: (0,) * len(shape), pipeline_mode=pl.Buffered(1))


def _ffn_call(x, norm_g, wg, wu, wd, attn=None):
    n = x.shape[0]
    tm = min(TOKEN_TILE, n)
    assert n % tm == 0
    row = lambda w: pl.BlockSpec((tm, w), lambda i: (i, 0))
    in_specs = [row(D_MODEL)]
    args = [x]
    if attn is not None:
        cm, cd, wom, wod = attn
        in_specs += [row(MLA_W), row(DIFF_W), _const_spec(wom.shape), _const_spec(wod.shape)]
        args += [cm, cd, wom, wod]
    in_specs += [_const_spec((1, D_MODEL)), _const_spec(wg.shape), _const_spec(wu.shape), _const_spec(wd.shape)]
    args += [norm_g.reshape(1, D_MODEL), wg, wu, wd]
    weights = 3 * D_MODEL * D_FF * 2 + (MLA_W + DIFF_W) * D_MODEL * 2
    tiles = tm * (4 * D_MODEL * 4 + 2 * D_FF * 4 + D_FF * 2 + 3 * D_MODEL * 4 + 2 * (MLA_W + DIFF_W) * 2)
    return pl.pallas_call(
        functools.partial(_ffn_kernel, with_attn=attn is not None),
        grid=(n // tm,),
        in_specs=in_specs,
        out_specs=row(D_MODEL),
        out_shape=jax.ShapeDtypeStruct((n, D_MODEL), F32),
        compiler_params=pltpu.CompilerParams(
            dimension_semantics=("arbitrary",), vmem_limit_bytes=_vmem_limit(weights + tiles)),
        name="ffn_attn" if attn is not None else "ffn",
    )(*args)


def _proj_kernel(*refs, sample):
    if sample:
        (x_ref, tab_ref, vec_ref, win_ref, wq_ref, b64_ref, wabs_ref,
         row_ref, dk_ref, dv_ref, qa_ref, dq_ref) = refs
    else:
        (x_ref, tab_ref, vec_ref, win_ref, wq_ref, b64_ref, wkv_ref,
         row_ref, dk_ref, dv_ref, qm_ref, km_ref, vm_ref, qd_ref, kd_ref, vd_ref) = refs
    x = x_ref[...]
    h = (_rms(x, D_MODEL) * vec_ref[0:1, :]).astype(BF16)
    z = _dot(h, win_ref[...])
    tab = tab_ref[...]
    cq_t, sq_t, ck_t, sk_t, cd_t, sd_t = (tab[:, i * LANE:(i + 1) * LANE] for i in range(6))
    gq = vec_ref[1:2, 384:512]
    gk = vec_ref[1:2, 512:640]

    ckv = _rms(z[:, _C_CKV:_C_CKV + KV_LORA], KV_LORA) * vec_ref[1:2, 256:384]
    krt = z[:, _C_KRA:_C_KRA + LANE] * ck_t + z[:, _C_KRB:_C_KRB + LANE] * sk_t
    row256 = jnp.concatenate([ckv, krt], axis=-1)
    row_ref[...] = row256[:, :MLA_ROW]

    c512 = jnp.concatenate([cd_t] * (DIFF_W // LANE), axis=-1)
    s512 = jnp.concatenate([sd_t] * (DIFF_W // LANE), axis=-1)

    def diff_norm(a, a_sw, g, g_sw):
        ss = _dot((a * a).astype(BF16), b64_ref[...])
        return lax.rsqrt(ss * (1.0 / DIFF_D) + EPS) * (a * g * c512 + a_sw * g_sw * s512)

    dq = diff_norm(z[:, _C_DQ:_C_DQ + DIFF_W], z[:, _C_DQS:_C_DQS + DIFF_W],
                   vec_ref[2:3, 0:DIFF_W], vec_ref[2:3, DIFF_W:2 * DIFF_W])
    dk = diff_norm(z[:, _C_DK:_C_DK + DIFF_W], z[:, _C_DKS:_C_DKS + DIFF_W],
                   vec_ref[3:4, 0:DIFF_W], vec_ref[3:4, DIFF_W:2 * DIFF_W])
    dv = z[:, _C_DV:_C_DV + DIFF_W]
    dk_ref[...] = dk
    dv_ref[...] = dv
    dq = dq * (DIFF_D ** -0.5 * LOG2E)

    cqn = (_rms(z[:, _C_CQ:_C_CQ + Q_LORA], Q_LORA) * vec_ref[1:2, 0:Q_LORA]).astype(BF16)
    q2 = _dot(cqn, wq_ref[...])
    if not sample:
        kv = _dot(row256.astype(BF16), wkv_ref[...])
    for hh in range(MLA_HEADS):
        sl = slice(hh * LANE, (hh + 1) * LANE)
        qh = q2[:, sl] * cq_t + q2[:, MLA_W + hh * LANE:MLA_W + (hh + 1) * LANE] * sq_t
        qn = _rms(qh, MLA_QK) * gq * (MLA_QK ** -0.5 * LOG2E)
        if sample:
            qa_ref[hh] = _dot((qn * gk).astype(BF16), wabs_ref[hh])
        else:
            qm_ref[hh] = qn.astype(BF16)
            km_ref[hh] = (_rms(kv[:, sl], MLA_QK) * gk).astype(BF16)
            vm_ref[hh] = kv[:, MLA_W + hh * LANE:MLA_W + (hh + 1) * LANE].astype(BF16)
    if sample:
        dq_ref[...] = dq
    else:
        lane = lax.broadcasted_iota(jnp.int32, (x.shape[0], LANE), 1)
        for hh in range(DIFF_HEADS):
            sl = slice(hh * LANE, (hh + 1) * LANE)
            qd_ref[hh, 0] = jnp.where(lane < DIFF_D, dq[:, sl], 0.0).astype(BF16)
            qd_ref[hh, 1] = jnp.where(lane >= DIFF_D, dq[:, sl], 0.0).astype(BF16)
            kd_ref[hh] = dk[:, sl].astype(BF16)
            vd_ref[hh] = dv[:, sl].astype(BF16)


def _proj_call(x, tab, lp, sample):
    n = x.shape[0]
    tm = min(TOKEN_TILE, n)
    assert n % tm == 0 and tab.shape[0] % tm == 0
    tab_blocks = tab.shape[0] // tm
    row = lambda w: pl.BlockSpec((tm, w), lambda i: (i, 0))
    const = lambda a: pl.BlockSpec(a.shape, lambda i: (0,) * a.ndim)
    heads = lambda nh: pl.BlockSpec((nh, tm, LANE), lambda i: (0, i, 0))
    in_specs = [row(D_MODEL), pl.BlockSpec((tm, 6 * LANE), lambda i: (i % tab_blocks, 0)),
                const(lp["vec"]), const(lp["w_in"]), const(lp["wq"]), const(lp["b64"])]
    args = [x, tab, lp["vec"], lp["w_in"], lp["wq"], lp["b64"]]
    out_specs = [row(MLA_ROW), row(DIFF_W), row(DIFF_W)]
    out_shape = [jax.ShapeDtypeStruct((n, MLA_ROW), F32), jax.ShapeDtypeStruct((n, DIFF_W), F32),
                 jax.ShapeDtypeStruct((n, DIFF_W), F32)]
    if sample:
        in_specs.append(const(lp["wabs"]))
        args.append(lp["wabs"])
        out_specs += [pl.BlockSpec((MLA_HEADS, tm, 2 * LANE), lambda i: (0, i, 0)), row(DIFF_W)]
        out_shape += [jax.ShapeDtypeStruct((MLA_HEADS, n, 2 * LANE), F32), jax.ShapeDtypeStruct((n, DIFF_W), F32)]
    else:
        in_specs.append(const(lp["wkv"]))
        args.append(lp["wkv"])
        out_specs += [heads(MLA_HEADS)] * 3 + [pl.BlockSpec((DIFF_HEADS, 2, tm, LANE), lambda i: (0, 0, i, 0))]
        out_specs += [heads(DIFF_HEADS)] * 2
        out_shape += [jax.ShapeDtypeStruct((MLA_HEADS, n, LANE), BF16)] * 3
        out_shape += [jax.ShapeDtypeStruct((DIFF_HEADS, 2, n, LANE), BF16)]
        out_shape += [jax.ShapeDtypeStruct((DIFF_HEADS, n, LANE), BF16)] * 2
    weights = 2 * 2 * (D_MODEL * IN_COLS_P + 2 * Q_LORA * 2 * MLA_W + DIFF_W * DIFF_W)
    tiles = tm * (IN_COLS_P * 4 * 2 + 4 * MLA_W * 4 + 8 * DIFF_W * 4 + 2 * (D_MODEL + 6 * LANE) * 4
                  + 2 * (MLA_ROW + 2 * DIFF_W) * 4 + 2 * (3 * MLA_W + 4 * DIFF_W) * 4)
    return pl.pallas_call(
        functools.partial(_proj_kernel, sample=sample),
        grid=(n // tm,),
        in_specs=in_specs,
        out_specs=out_specs,
        out_shape=out_shape,
        compiler_params=pltpu.CompilerParams(
            dimension_semantics=("arbitrary",), vmem_limit_bytes=_vmem_limit(weights + tiles)),
        name="proj_sample" if sample else "proj_prompt",
    )(*args)


def _diff_finish(o1, o2, par):
    d = o1 - par[0:1, :] * o2
    return _rms(d, DIFF_HD) * par[1:2, :]


def _flash_kernel(*refs, nmaps, heads, blk):
    if nmaps == 2:
        q_ref, k_ref, v_ref, par_ref, o_ref = refs
    else:
        q_ref, k_ref, v_ref, o_ref = refs
    i = pl.program_id(2)
    m_rows = nmaps * blk

    def step(j, carries, diagonal):
        start = pl.multiple_of(j * blk, blk)
        out = []
        for hh in range(heads):
            m, l, acc = carries[hh]
            s = _dot_nt(q_ref[hh].reshape(m_rows, LANE), k_ref[hh, pl.ds(start, blk), :])
            if diagonal:
                r = lax.broadcasted_iota(jnp.int32, (m_rows, blk), 0)
                c = lax.broadcasted_iota(jnp.int32, (m_rows, blk), 1)
                if nmaps == 2:
                    r = jnp.where(r >= blk, r - blk, r)
                s = jnp.where(c <= r, s, NEG)
            m_new = jnp.maximum(m, jnp.max(s, axis=-1, keepdims=True))
            p = jnp.exp2(s - m_new)
            corr = jnp.exp2(m - m_new)
            l = l * corr + jnp.sum(p, axis=-1, keepdims=True)
            acc = acc * corr + _dot(p.astype(BF16), v_ref[hh, pl.ds(start, blk), :])
            out.append((m_new, l, acc))
        return tuple(out)

    init = tuple((jnp.full((m_rows, 1), NEG, F32), jnp.zeros((m_rows, 1), F32), jnp.zeros((m_rows, LANE), F32))
                 for _ in range(heads))
    carries = lax.fori_loop(0, i, lambda j, c: step(j, c, False), init)
    fin = step(i, carries, True)
    for hh in range(heads):
        o = fin[hh][2] / fin[hh][1]
        if nmaps == 2:
            o = _diff_finish(o[:blk], o[blk:], par_ref[...])
        o_ref[:, hh * LANE:(hh + 1) * LANE] = o.astype(o_ref.dtype)


def _flash_call(q, k, v, batch, seq, heads, par=None):
    nmaps = 2 if q.ndim == 4 else 1
    units = q.shape[0]
    blk = min(ATTN_TILE, seq)
    assert seq % blk == 0 and units % heads == 0
    nq = seq // blk
    if nmaps == 2:
        q_spec = pl.BlockSpec((heads, 2, blk, LANE), lambda b, u, i: (u, 0, b * nq + i, 0))
    else:
        q_spec = pl.BlockSpec((heads, blk, LANE), lambda b, u, i: (u, b * nq + i, 0))
    kv_spec = pl.BlockSpec((heads, seq, LANE), lambda b, u, i: (u, b, 0))
    in_specs = [q_spec, kv_spec, kv_spec]
    args = [q, k, v]
    if nmaps == 2:
        in_specs.append(pl.BlockSpec(par.shape, lambda b, u, i: (0, 0)))
        args.append(par)
    need = 2 * 2 * heads * seq * LANE * 2 + heads * nmaps * blk * (7 * blk * 4 + 8 * LANE * 4)
    return pl.pallas_call(
        functools.partial(_flash_kernel, nmaps=nmaps, heads=heads, blk=blk),
        grid=(batch, units // heads, nq),
        in_specs=in_specs,
        out_specs=pl.BlockSpec((blk, heads * LANE), lambda b, u, i: (b * nq + i, u)),
        out_shape=jax.ShapeDtypeStruct((batch * seq, units * LANE), BF16),
        compiler_params=pltpu.CompilerParams(
            dimension_semantics=("arbitrary", "arbitrary", "arbitrary"), vmem_limit_bytes=_vmem_limit(need)),
        name="flash_diff" if nmaps == 2 else "flash_mla",
    )(*args)


def _decode_kernel(pt_ref, cm_hbm, ck_hbm, cv_hbm, qa_ref, qd_ref, rown_ref, kn_ref, vn_ref, wuk_ref,
                   om_ref, od_ref,
                   mbuf, kbuf, vbuf, sems, mm_ref, lm_ref, am_ref, md_ref, ld_ref, ad_ref,
                   *, page, pages_per_step, page_base, sample_len):
    b = pl.program_id(0)
    c = pl.program_id(1)
    nchunk = pl.num_programs(1)
    step = b * nchunk + c
    slot = step % 2
    tok = page * pages_per_step
    ncols = MLA_HEADS * sample_len

    def copies(bb, cc, sl):
        out = []
        for j in range(pages_per_step):
            pg = pt_ref[bb, cc * pages_per_step + j] + page_base
            out.append(pltpu.make_async_copy(
                cm_hbm.at[pl.ds(pl.multiple_of(pg * MLA_ROW, MLA_ROW), MLA_ROW)],
                mbuf.at[sl, :, pl.ds(j * page, page)], sems.at[sl, 0]))
            rows = pl.ds(pl.multiple_of(pg * (page * DIFF_HEADS), page * DIFF_HEADS), page * DIFF_HEADS)
            dst = pl.ds(j * page * DIFF_HEADS, page * DIFF_HEADS)
            out.append(pltpu.make_async_copy(ck_hbm.at[rows], kbuf.at[sl, dst], sems.at[sl, 1]))
            out.append(pltpu.make_async_copy(cv_hbm.at[rows], vbuf.at[sl, dst], sems.at[sl, 2]))
        return out

    @pl.when(step == 0)
    def _():
        for cp in copies(0, 0, 0):
            cp.start()

    @pl.when(step + 1 < pl.num_programs(0) * nchunk)
    def _():
        nxt = step + 1
        for cp in copies(nxt // nchunk, nxt % nchunk, 1 - slot):
            cp.start()

    @pl.when(c == 0)
    def _():
        mm_ref[...] = jnp.full(mm_ref.shape, NEG, F32)
        md_ref[...] = jnp.full(md_ref.shape, NEG, F32)
        lm_ref[...] = jnp.zeros(lm_ref.shape, F32)
        ld_ref[...] = jnp.zeros(ld_ref.shape, F32)
        am_ref[...] = jnp.zeros(am_ref.shape, F32)
        ad_ref[...] = jnp.zeros(ad_ref.shape, F32)

    for cp in copies(b, c, slot):
        cp.wait()

    def online(s, pv_fn, m_ref, l_ref, a_ref):
        m_old = m_ref[...]
        m_new = jnp.maximum(m_old, jnp.max(s, axis=1, keepdims=True))
        p = jnp.exp2(s - m_new)
        corr = jnp.exp2(m_old - m_new)
        l_ref[...] = l_ref[...] * corr + jnp.sum(p, axis=1, keepdims=True)
        a_ref[...] = a_ref[...] * corr + pv_fn(p.astype(BF16))
        m_ref[...] = m_new

    def mla_scores(row_t):
        n = row_t.shape[1]
        rowb = row_t.astype(BF16)
        lhs = jnp.concatenate([wuk_ref[...], qa_ref[0]], axis=0)
        y = _dot(lhs, rowb)
        kn = y[:MLA_HEADS * MLA_NOPE]
        ss = jnp.sum((kn * kn).reshape(MLA_NOPE, MLA_HEADS, n), axis=0)
        kr = row_t[KV_LORA:MLA_ROW]
        ss = ss + jnp.sum(kr * kr, axis=0, keepdims=True)
        rs = lax.rsqrt(ss * (1.0 / MLA_QK) + EPS)
        s = y[MLA_HEADS * MLA_NOPE:] * jnp.concatenate([rs] * sample_len, axis=0)
        return s, rowb[:KV_LORA]

    def heads_of(ref_fn):
        return jnp.concatenate([ref_fn(hh) for hh in range(DIFF_HEADS)], axis=-1).astype(BF16)

    s_m, cb = mla_scores(mbuf[slot])
    online(s_m, lambda p: _dot_nt(p, cb), mm_ref, lm_ref, am_ref)
    kc = heads_of(lambda hh: kbuf[slot, pl.ds(hh, tok, stride=DIFF_HEADS), :])
    vc = heads_of(lambda hh: vbuf[slot, pl.ds(hh, tok, stride=DIFF_HEADS), :])
    s_d = _dot_nt(qd_ref[0], kc)
    online(s_d, lambda p: _dot(p, vc), md_ref, ld_ref, ad_ref)

    @pl.when(c == nchunk - 1)
    def _():
        def mask(s, t_of_row):
            col = lax.broadcasted_iota(jnp.int32, s.shape, 1)
            row = lax.broadcasted_iota(jnp.int32, s.shape, 0)
            return jnp.where(col <= t_of_row(row), s, NEG)

        sn_m, cnb = mla_scores(rown_ref[0])
        online(mask(sn_m, lambda r: r // MLA_HEADS), lambda p: _dot_nt(p, cnb), mm_ref, lm_ref, am_ref)
        knb = kn_ref[0].astype(BF16)
        vnb = vn_ref[0].astype(BF16)
        sn_d = _dot_nt(qd_ref[0], knb)
        online(mask(sn_d, lambda r: r % sample_len), lambda p: _dot(p, vnb), md_ref, ld_ref, ad_ref)
        om_ref[0] = am_ref[...] / lm_ref[...]
        l_d = ld_ref[...]
        rows_per_head = 2 * sample_len
        for hh in range(DIFF_HEADS):
            rsl = slice(hh * rows_per_head, (hh + 1) * rows_per_head)
            od_ref[0, rsl, :] = ad_ref[rsl, hh * LANE:(hh + 1) * LANE] / l_d[rsl]


def _decode_call(page_table, cm_rows, ck_rows, cv_rows, qa, qd, row_new, k_new, v_new, wuk, page, page_base):
    bsz, n_pages = page_table.shape
    pps = min(DECODE_PAGES, n_pages)
    assert n_pages % pps == 0
    nchunk = n_pages // pps
    ncols = qa.shape[1]
    sample_len = ncols // MLA_HEADS
    tok = page * pps
    per_b = lambda shape: pl.BlockSpec((1,) + shape, lambda b, c, pt: (b, 0, 0))
    grid_spec = pltpu.PrefetchScalarGridSpec(
        num_scalar_prefetch=1,
        grid=(bsz, nchunk),
        in_specs=[pl.BlockSpec(memory_space=pl.ANY)] * 3 + [
            per_b((ncols, MLA_ROW)), per_b((ncols, DIFF_W)),
            per_b((MLA_ROW, page)), per_b((NEW_PAD, DIFF_W)), per_b((NEW_PAD, DIFF_W)),
            pl.BlockSpec(wuk.shape, lambda b, c, pt: (0, 0))],
        out_specs=[per_b((ncols, LANE)), per_b((ncols, LANE))],
        scratch_shapes=[
            pltpu.VMEM((2, MLA_ROW, tok), F32),
            pltpu.VMEM((2, tok * DIFF_HEADS, LANE), F32),
            pltpu.VMEM((2, tok * DIFF_HEADS, LANE), F32),
            pltpu.SemaphoreType.DMA((2, 3)),
            pltpu.VMEM((ncols, 1), F32), pltpu.VMEM((ncols, 1), F32), pltpu.VMEM((ncols, KV_LORA), F32),
            pltpu.VMEM((ncols, 1), F32), pltpu.VMEM((ncols, 1), F32), pltpu.VMEM((ncols, DIFF_W), F32),
        ])
    need = 2 * tok * (MLA_ROW + 2 * DIFF_W) * 4 + tok * (2 * (DIFF_W + MLA_ROW) * 2 + (MLA_HEADS * MLA_NOPE + 4 * ncols) * 4 * 2)
    return pl.pallas_call(
        functools.partial(_decode_kernel, page=page, pages_per_step=pps, page_base=page_base,
                          sample_len=sample_len),
        grid_spec=grid_spec,
        out_shape=[jax.ShapeDtypeStruct((bsz, ncols, LANE), F32)] * 2,
        compiler_params=pltpu.CompilerParams(
            dimension_semantics=("arbitrary", "arbitrary"), vmem_limit_bytes=_vmem_limit(need)),
        name="decode",
    )(page_table, cm_rows, ck_rows, cv_rows, qa, qd, row_new, k_new, v_new, wuk)


def _sample_combine_kernel(om_ref, od_ref, wuv_ref, par_ref, cm_ref, cd_ref):
    for hh in range(MLA_HEADS):
        cm_ref[:, hh * LANE:(hh + 1) * LANE] = _dot(om_ref[hh].astype(BF16), wuv_ref[hh]).astype(BF16)
    for hh in range(DIFF_HEADS):
        cd_ref[:, hh * LANE:(hh + 1) * LANE] = _diff_finish(od_ref[hh, 0], od_ref[hh, 1], par_ref[...]).astype(BF16)


def _sample_combine_call(om, od, wuv, par):
    n = om.shape[1]
    full = lambda a: pl.BlockSpec(a.shape, lambda i: (0,) * a.ndim)
    return pl.pallas_call(
        _sample_combine_kernel,
        grid=(1,),
        in_specs=[full(om), full(od), full(wuv), full(par)],
        out_specs=[pl.BlockSpec((n, MLA_W), lambda i: (0, 0)), pl.BlockSpec((n, DIFF_W), lambda i: (0, 0))],
        out_shape=[jax.ShapeDtypeStruct((n, MLA_W), BF16), jax.ShapeDtypeStruct((n, DIFF_W), BF16)],
        name="sample_combine",
    )(om, od, wuv, par)


def _rot_half_cols(w, group, half):
    w3 = w.reshape(w.shape[0], -1, group)
    zeros = jnp.zeros_like(w3[..., 2 * half:])
    return jnp.concatenate([-w3[..., half:2 * half], w3[..., :half], zeros], axis=-1).reshape(w.shape)


def _rot_half_gain(g, half):
    return jnp.concatenate([g[half:2 * half], g[:half], jnp.zeros_like(g[2 * half:])])


def _prep_layer(w_in, mix_norm, mla_q_norm, mla_kv_norm, w_uq, w_uk, w_uv, mla_q_gain, mla_k_gain,
                diff_q_gain, diff_k_gain):
    o1 = Q_LORA
    o2 = o1 + KV_LORA
    o3 = o2 + MLA_ROPE
    o4 = o3 + DIFF_W
    o5 = o4 + DIFF_W
    wkr = w_in[:, o2:o3]
    wdq, wdk, wdv = w_in[:, o3:o4], w_in[:, o4:o5], w_in[:, o5:]
    pad = jnp.zeros((D_MODEL, LANE - MLA_ROPE), F32)
    w_in_p = jnp.concatenate(
        [w_in[:, :o1], w_in[:, o1:o2], wdq, wdk, wdv,
         _rot_half_cols(wdq, DIFF_D, DIFF_ROT // 2), _rot_half_cols(wdk, DIFF_D, DIFF_ROT // 2),
         wkr, pad, _rot_half_cols(wkr, MLA_ROPE, MLA_ROPE // 2), pad], axis=1).astype(BF16)
    assert w_in_p.shape[1] == IN_COLS_P

    hpad = jnp.zeros((Q_LORA, MLA_HEADS, LANE - MLA_QK), F32)
    wq = jnp.concatenate([w_uq, hpad], axis=-1).reshape(Q_LORA, MLA_W)
    wq_rope = w_uq[..., MLA_NOPE:]
    wq_sw = jnp.concatenate(
        [jnp.zeros((Q_LORA, MLA_HEADS, MLA_NOPE), F32),
         _rot_half_cols(wq_rope.reshape(Q_LORA, -1), MLA_ROPE, MLA_ROPE // 2).reshape(Q_LORA, MLA_HEADS, MLA_ROPE),
         hpad], axis=-1).reshape(Q_LORA, MLA_W)
    wq2 = jnp.concatenate([wq, wq_sw], axis=1).astype(BF16)

    vpad = jnp.zeros((KV_LORA, MLA_HEADS, LANE - MLA_NOPE), F32)
    wk_c = jnp.concatenate([w_uk, vpad], axis=-1).reshape(KV_LORA, MLA_W)
    place = jnp.concatenate([jnp.zeros((MLA_ROPE, MLA_NOPE), F32), jnp.eye(MLA_ROPE, dtype=F32),
                             jnp.zeros((MLA_ROPE, LANE - MLA_QK), F32)], axis=1)
    wk_r = jnp.tile(place, (1, MLA_HEADS))
    wk = jnp.concatenate([wk_c, wk_r, jnp.zeros((LANE - MLA_ROPE, MLA_W), F32)], axis=0)
    wv = jnp.concatenate([jnp.concatenate([w_uv, vpad], axis=-1).reshape(KV_LORA, MLA_W),
                          jnp.zeros((LANE, MLA_W), F32)], axis=0)
    wkv = jnp.concatenate([wk, wv], axis=1).astype(BF16)

    wabs_top = jnp.concatenate([jnp.transpose(w_uk, (1, 2, 0)), jnp.zeros((MLA_HEADS, MLA_NOPE, LANE), F32)], axis=-1)
    wabs_mid = jnp.concatenate([jnp.zeros((MLA_ROPE, LANE), F32), jnp.eye(MLA_ROPE, dtype=F32),
                                jnp.zeros((MLA_ROPE, LANE - MLA_ROPE), F32)], axis=1)
    wabs = jnp.concatenate([wabs_top, jnp.broadcast_to(wabs_mid, (MLA_HEADS,) + wabs_mid.shape),
                            jnp.zeros((MLA_HEADS, LANE - MLA_QK, 2 * LANE), F32)], axis=1).astype(BF16)

    wuk_t = jnp.concatenate([jnp.transpose(w_uk, (2, 1, 0)).reshape(MLA_HEADS * MLA_NOPE, KV_LORA),
                             jnp.zeros((MLA_HEADS * MLA_NOPE, MLA_ROPE), F32)], axis=1).astype(BF16)
    wuv_pad = jnp.concatenate([jnp.transpose(w_uv, (1, 0, 2)), jnp.zeros((MLA_HEADS, KV_LORA, LANE - MLA_V), F32)],
                              axis=-1).astype(BF16)

    qpad = jnp.zeros((LANE - MLA_QK,), F32)
    row1 = jnp.concatenate([mla_q_norm, mla_kv_norm, mla_q_gain, qpad, mla_k_gain, qpad,
                            jnp.zeros((D_MODEL - Q_LORA - KV_LORA - 2 * LANE,), F32)])
    reps = DIFF_W // DIFF_D
    row2 = jnp.concatenate([jnp.tile(diff_q_gain, reps), jnp.tile(_rot_half_gain(diff_q_gain, DIFF_ROT // 2), reps)])
    row3 = jnp.concatenate([jnp.tile(diff_k_gain, reps), jnp.tile(_rot_half_gain(diff_k_gain, DIFF_ROT // 2), reps)])
    vec = jnp.stack([mix_norm, row1, row2, row3] + [jnp.zeros((D_MODEL,), F32)] * 4)
    b64 = jnp.kron(jnp.eye(DIFF_W // DIFF_D, dtype=F32), jnp.ones((DIFF_D, DIFF_D), F32)).astype(BF16)
    return dict(w_in=w_in_p, wq=wq2, wkv=wkv, wabs=wabs, wuk_t=wuk_t, wuv_pad=wuv_pad, vec=vec, b64=b64)


def _rope_tables(pos):
    n = pos.shape[0]

    def cs(r):
        half = r // 2
        inv = jnp.power(jnp.float32(ROPE_THETA), -jnp.arange(half, dtype=F32) * (2.0 / r))
        ang = pos.astype(F32)[:, None] * inv[None, :]
        return jnp.cos(ang), jnp.sin(ang)

    cm, sm = cs(MLA_ROPE)
    cd, sd = cs(DIFF_ROT)
    ones = lambda w: jnp.ones((n, w), F32)
    zeros = lambda w: jnp.zeros((n, w), F32)
    c64 = jnp.concatenate([cd, cd, ones(DIFF_D - DIFF_ROT)], axis=1)
    s64 = jnp.concatenate([sd, sd, zeros(DIFF_D - DIFF_ROT)], axis=1)
    return jnp.concatenate([
        ones(MLA_NOPE), cm, cm, zeros(LANE - MLA_QK),
        zeros(MLA_NOPE), sm, sm, zeros(LANE - MLA_QK),
        cm, cm, zeros(LANE - MLA_ROPE),
        sm, sm, zeros(LANE - MLA_ROPE),
        c64, c64, s64, s64], axis=1)


def kernel(x_prompt, x_sample, cache_mla, cache_diff_k, cache_diff_v, page_table, ffn1_norm, ffn1_w_gate, ffn1_w_up, ffn1_w_down, mix_norm, w_in, mla_q_norm, mla_kv_norm, w_uq, w_uk, w_uv, mla_q_gain, mla_k_gain, diff_q_gain, diff_k_gain, diff_lq1, diff_lk1, diff_lq2, diff_lk2, diff_sub_gain, w_o, ffn2_norm, ffn2_w_gate, ffn2_w_up, ffn2_w_down):
    depth = w_in.shape[0]
    bp, seq, _ = x_prompt.shape
    bs, ts, _ = x_sample.shape
    n_pool, page = cache_mla.shape[1], cache_mla.shape[2]
    past_len = page_table.shape[1] * page
    assert ts <= NEW_PAD and MLA_HEADS * ts == 2 * DIFF_HEADS * ts

    cm_rows = jnp.swapaxes(cache_mla, 2, 3).reshape(-1, page)
    ck_rows = cache_diff_k.reshape(-1, DIFF_HD)
    cv_rows = cache_diff_v.reshape(-1, DIFF_HD)

    tab_p = _rope_tables(jnp.arange(seq, dtype=jnp.int32))
    tab_s = jnp.tile(_rope_tables(past_len + jnp.arange(ts, dtype=jnp.int32)), (bs, 1))

    yp = x_prompt.reshape(bp * seq, D_MODEL)
    ys = x_sample.reshape(bs * ts, D_MODEL)
    outs = [[] for _ in range(6)]
    for l in range(depth):
        lp = _prep_layer(w_in[l], mix_norm[l], mla_q_norm[l], mla_kv_norm[l], w_uq[l], w_uk[l], w_uv[l],
                         mla_q_gain[l], mla_k_gain[l], diff_q_gain[l], diff_k_gain[l])
        lam_init = 0.8 - 0.6 * math.exp(-0.3 * l)
        lam = (jnp.exp(jnp.sum(diff_lq1[l] * diff_lk1[l])) - jnp.exp(jnp.sum(diff_lq2[l] * diff_lk2[l])) + lam_init)
        par = jnp.stack([jnp.broadcast_to(lam, (DIFF_HD,)), diff_sub_gain[l] * (1.0 - lam_init)]
                        + [jnp.zeros((DIFF_HD,), F32)] * 6)
        f1 = (ffn1_w_gate[l].astype(BF16), ffn1_w_up[l].astype(BF16), ffn1_w_down[l].astype(BF16))
        f2 = (ffn2_w_gate[l].astype(BF16), ffn2_w_up[l].astype(BF16), ffn2_w_down[l].astype(BF16))
        wo_m = jnp.concatenate([w_o[l, :MLA_HEADS * MLA_V].reshape(MLA_HEADS, MLA_V, D_MODEL),
                                jnp.zeros((MLA_HEADS, LANE - MLA_V, D_MODEL), F32)], axis=1
                               ).reshape(MLA_W, D_MODEL).astype(BF16)
        wo_d = w_o[l, MLA_HEADS * MLA_V:].astype(BF16)

        x1p = _ffn_call(yp, ffn1_norm[l], *f1)
        row_p, dk_p, dv_p, qm, km, vm, qd, kd, vd = _proj_call(x1p, tab_p, lp, sample=False)
        cat_m = _flash_call(qm, km, vm, bp, seq, MLA_HEADS_PER_STEP)
        cat_d = _flash_call(qd, kd, vd, bp, seq, 1, par=par)
        yp = _ffn_call(x1p, ffn2_norm[l], *f2, attn=(cat_m, cat_d, wo_m, wo_d))

        x1s = _ffn_call(ys, ffn1_norm[l], *f1)
        row_s, dk_s, dv_s, qabs, dq_s = _proj_call(x1s, tab_s, lp, sample=True)
        qa = jnp.transpose(qabs[:, :, :MLA_ROW].reshape(MLA_HEADS, bs, ts, MLA_ROW), (1, 2, 0, 3))
        qa = qa.reshape(bs, ts * MLA_HEADS, MLA_ROW).astype(BF16)
        dq5 = dq_s.reshape(bs, ts, DIFF_HEADS, 2, DIFF_D)
        qbd = jnp.einsum("bthmd,hH,mM->bhmtHMd", dq5, jnp.eye(DIFF_HEADS, dtype=F32), jnp.eye(2, dtype=F32))
        qbd = qbd.reshape(bs, DIFF_HEADS * 2 * ts, DIFF_W).astype(BF16)
        padn = lambda a: jnp.pad(a.reshape(bs, ts, -1), ((0, 0), (0, NEW_PAD - ts), (0, 0)))
        row_new = jnp.pad(jnp.swapaxes(row_s.reshape(bs, ts, MLA_ROW), 1, 2), ((0, 0), (0, 0), (0, page - ts)))
        om, od = _decode_call(page_table, cm_rows, ck_rows, cv_rows, qa, qbd, row_new, padn(dk_s), padn(dv_s),
                              lp["wuk_t"], page, l * n_pool)
        om = jnp.transpose(om.reshape(bs, ts, MLA_HEADS, KV_LORA), (2, 0, 1, 3)).reshape(MLA_HEADS, bs * ts, KV_LORA)
        od = jnp.transpose(od.reshape(bs, DIFF_HEADS, 2, ts, DIFF_HD), (1, 2, 0, 3, 4)
                           ).reshape(DIFF_HEADS, 2, bs * ts, DIFF_HD)
        cat_ms, cat_ds = _sample_combine_call(om, od, lp["wuv_pad"], par)
        ys = _ffn_call(x1s, ffn2_norm[l], *f2, attn=(cat_ms, cat_ds, wo_m, wo_d))

        outs[0].append(row_p.reshape(bp, seq, MLA_ROW))
        outs[1].append(dk_p.reshape(bp, seq, DIFF_HEADS, DIFF_HD))
        outs[2].append(dv_p.reshape(bp, seq, DIFF_HEADS, DIFF_HD))
        outs[3].append(row_s.reshape(bs, ts, MLA_ROW))
        outs[4].append(dk_s.reshape(bs, ts, DIFF_HEADS, DIFF_HD))
        outs[5].append(dv_s.reshape(bs, ts, DIFF_HEADS, DIFF_HD))
    return (yp.reshape(bp, seq, D_MODEL), ys.reshape(bs, ts, D_MODEL)) + tuple(jnp.stack(o) for o in outs)
```

```python
import functools
import math

import jax
import jax.numpy as jnp
from jax import lax
from jax.experimental import pallas as pl
from jax.experimental.pallas import tpu as pltpu

F32 = jnp.float32
BF16 = jnp.bfloat16

D_MODEL = 1024
MLA_HEADS = 8
MLA_V = 64
MLA_NOPE = 64
MLA_ROPE = 32
MLA_QK = MLA_NOPE + MLA_ROPE
Q_LORA = 256
KV_LORA = 128
MLA_ROW = KV_LORA + MLA_ROPE
DIFF_D = 64
DIFF_HD = 2 * DIFF_D
DIFF_HEADS = 4
DIFF_ROT = DIFF_D // 4
D_FF = 2816
ROPE_THETA = 500000.0
EPS = 1e-6
NEG = -1e30
LOG2E = 1.4426950408889634

LANE = 128
V7X_VMEM_BYTES = 64 * 1024 * 1024

TOKEN_TILE = 512
ATTN_TILE = 1024
MLA_HEADS_PER_STEP = 2
DECODE_PAGES = 16
NEW_PAD = 16

_C_CQ = 0
_C_CKV = _C_CQ + Q_LORA
_C_DQ = _C_CKV + KV_LORA
_C_DK = _C_DQ + DIFF_HEADS * DIFF_HD
_C_DV = _C_DK + DIFF_HEADS * DIFF_HD
_C_DQS = _C_DV + DIFF_HEADS * DIFF_HD
_C_DKS = _C_DQS + DIFF_HEADS * DIFF_HD
_C_KRA = _C_DKS + DIFF_HEADS * DIFF_HD
_C_KRB = _C_KRA + LANE
IN_COLS_P = _C_KRB + LANE
DIFF_W = DIFF_HEADS * DIFF_HD
MLA_W = MLA_HEADS * LANE


def _vmem_limit(nbytes):
    return int(min(max(nbytes, 32 * 1024 * 1024), V7X_VMEM_BYTES - 8 * 1024 * 1024))


def _rms(x, n):
    return x * lax.rsqrt(jnp.sum(x * x, axis=-1, keepdims=True) * (1.0 / n) + EPS)


def _dot(a, b):
    return jnp.dot(a, b, preferred_element_type=F32)


def _dot_nt(a, b):
    return lax.dot_general(a, b, (((1,), (1,)), ((), ())), preferred_element_type=F32)


def _ffn_kernel(*refs, with_attn):
    if with_attn:
        x_ref, cm_ref, cd_ref, wom_ref, wod_ref, g_ref, wg_ref, wu_ref, wd_ref, o_ref = refs
        x = x_ref[...] + _dot(cm_ref[...], wom_ref[...]) + _dot(cd_ref[...], wod_ref[...])
    else:
        x_ref, g_ref, wg_ref, wu_ref, wd_ref, o_ref = refs
        x = x_ref[...]
    h = (_rms(x, D_MODEL) * g_ref[...]).astype(BF16)
    g = _dot(h, wg_ref[...])
    u = _dot(h, wu_ref[...])
    a = (g * (1.0 / (1.0 + jnp.exp(-g))) * u).astype(BF16)
    o_ref[...] = x + 0.5 * _dot(a, wd_ref[...])


def _const_spec(shape):
    return pl.BlockSpec(shape, lambda i: (0,) * len(shape), pipeline_mode=pl.Buffered(1))


def _ffn_call(x, norm_g, wg, wu, wd, attn=None):
    n = x.shape[0]
    tm = min(TOKEN_TILE, n)
    assert n % tm == 0
    row = lambda w: pl.BlockSpec((tm, w), lambda i: (i, 0))
    in_specs = [row(D_MODEL)]
    args = [x]
    if attn is not None:
        cm, cd, wom, wod = attn
        in_specs += [row(MLA_W), row(DIFF_W), _const_spec(wom.shape), _const_spec(wod.shape)]
        args += [cm, cd, wom, wod]
    in_specs += [_const_spec((1, D_MODEL)), _const_spec(wg.shape), _const_spec(wu.shape), _const_spec(wd.shape)]
    args += [norm_g.reshape(1, D_MODEL), wg, wu, wd]
    weights = 3 * D_MODEL * D_FF * 2 + (MLA_W + DIFF_W) * D_MODEL * 2
    tiles = tm * (4 * D_MODEL * 4 + 2 * D_FF * 4 + D_FF * 2 + 3 * D_MODEL * 4 + 2 * (MLA_W + DIFF_W) * 2)
    return pl.pallas_call(
        functools.partial(_ffn_kernel, with_attn=attn is not None),
        grid=(n // tm,),
        in_specs=in_specs,
        out_specs=row(D_MODEL),
        out_shape=jax.ShapeDtypeStruct((n, D_MODEL), F32),
        compiler_params=pltpu.CompilerParams(
            dimension_semantics=("arbitrary",), vmem_limit_bytes=_vmem_limit(weights + tiles)),
        name="ffn_attn" if attn is not None else "ffn",
    )(*args)


def _proj_kernel(*refs, sample):
    if sample:
        (x_ref, tab_ref, vec_ref, win_ref, wq_ref, b64_ref, wabs_ref,
         row_ref, dk_ref, dv_ref, qa_ref, dq_ref) = refs
    else:
        (x_ref, tab_ref, vec_ref, win_ref, wq_ref, b64_ref, wkv_ref,
         row_ref, dk_ref, dv_ref, qm_ref, km_ref, vm_ref, qd_ref, kd_ref, vd_ref) = refs
    x = x_ref[...]
    h = (_rms(x, D_MODEL) * vec_ref[0:1, :]).astype(BF16)
    z = _dot(h, win_ref[...])
    tab = tab_ref[...]
    cq_t, sq_t, ck_t, sk_t, cd_t, sd_t = (tab[:, i * LANE:(i + 1) * LANE] for i in range(6))
    gq = vec_ref[1:2, 384:512]
    gk = vec_ref[1:2, 512:640]

    ckv = _rms(z[:, _C_CKV:_C_CKV + KV_LORA], KV_LORA) * vec_ref[1:2, 256:384]
    krt = z[:, _C_KRA:_C_KRA + LANE] * ck_t + z[:, _C_KRB:_C_KRB + LANE] * sk_t
    row256 = jnp.concatenate([ckv, krt], axis=-1)
    row_ref[...] = row256[:, :MLA_ROW]

    c512 = jnp.concatenate([cd_t] * (DIFF_W // LANE), axis=-1)
    s512 = jnp.concatenate([sd_t] * (DIFF_W // LANE), axis=-1)

    def diff_norm(a, a_sw, g, g_sw):
        ss = _dot((a * a).astype(BF16), b64_ref[...])
        return lax.rsqrt(ss * (1.0 / DIFF_D) + EPS) * (a * g * c512 + a_sw * g_sw * s512)

    dq = diff_norm(z[:, _C_DQ:_C_DQ + DIFF_W], z[:, _C_DQS:_C_DQS + DIFF_W],
                   vec_ref[2:3, 0:DIFF_W], vec_ref[2:3, DIFF_W:2 * DIFF_W])
    dk = diff_norm(z[:, _C_DK:_C_DK + DIFF_W], z[:, _C_DKS:_C_DKS + DIFF_W],
                   vec_ref[3:4, 0:DIFF_W], vec_ref[3:4, DIFF_W:2 * DIFF_W])
    dv = z[:, _C_DV:_C_DV + DIFF_W]
    tm = x.shape[0]
    for hh in range(DIFF_HEADS):
        dk_ref[pl.ds(hh, tm, stride=DIFF_HEADS), :] = dk[:, hh * LANE:(hh + 1) * LANE]
        dv_ref[pl.ds(hh, tm, stride=DIFF_HEADS), :] = dv[:, hh * LANE:(hh + 1) * LANE]
    dq = dq * (DIFF_D ** -0.5 * LOG2E)
    lane = lax.broadcasted_iota(jnp.int32, (tm, LANE), 1)

    cqn = (_rms(z[:, _C_CQ:_C_CQ + Q_LORA], Q_LORA) * vec_ref[1:2, 0:Q_LORA]).astype(BF16)
    q2 = _dot(cqn, wq_ref[...])
    if not sample:
        kv = _dot(row256.astype(BF16), wkv_ref[...])
    for hh in range(MLA_HEADS):
        sl = slice(hh * LANE, (hh + 1) * LANE)
        qh = q2[:, sl] * cq_t + q2[:, MLA_W + hh * LANE:MLA_W + (hh + 1) * LANE] * sq_t
        qn = _rms(qh, MLA_QK) * gq * (MLA_QK ** -0.5 * LOG2E)
        if sample:
            qa_ref[hh] = _dot((qn * gk).astype(BF16), wabs_ref[hh])
        else:
            qm_ref[hh] = qn.astype(BF16)
            km_ref[hh] = (_rms(kv[:, sl], MLA_QK) * gk).astype(BF16)
            vh = kv[:, MLA_W + hh * LANE:MLA_W + (hh + 1) * LANE]
            vm_ref[hh] = jnp.where(lane == MLA_V, 1.0, vh).astype(BF16)
    if sample:
        dq_ref[...] = dq
    else:
        for hh in range(DIFF_HEADS):
            sl = slice(hh * LANE, (hh + 1) * LANE)
            qd_ref[hh, 0] = jnp.where(lane < DIFF_D, dq[:, sl], 0.0).astype(BF16)
            qd_ref[hh, 1] = jnp.where(lane >= DIFF_D, dq[:, sl], 0.0).astype(BF16)
            kd_ref[hh] = dk[:, sl].astype(BF16)
            vd_ref[hh] = dv[:, sl].astype(BF16)


def _proj_call(x, tab, lp, sample):
    n = x.shape[0]
    tm = min(TOKEN_TILE, n)
    assert n % tm == 0 and tab.shape[0] % tm == 0
    tab_blocks = tab.shape[0] // tm
    row = lambda w: pl.BlockSpec((tm, w), lambda i: (i, 0))
    const = lambda a: pl.BlockSpec(a.shape, lambda i: (0,) * a.ndim)
    heads = lambda nh: pl.BlockSpec((nh, tm, LANE), lambda i: (0, i, 0))
    in_specs = [row(D_MODEL), pl.BlockSpec((tm, 6 * LANE), lambda i: (i % tab_blocks, 0)),
                const(lp["vec"]), const(lp["w_in"]), const(lp["wq"]), const(lp["b64"])]
    args = [x, tab, lp["vec"], lp["w_in"], lp["wq"], lp["b64"]]
    head_rows = pl.BlockSpec((tm * DIFF_HEADS, DIFF_HD), lambda i: (i, 0))
    out_specs = [row(MLA_ROW), head_rows, head_rows]
    out_shape = [jax.ShapeDtypeStruct((n, MLA_ROW), F32), jax.ShapeDtypeStruct((n * DIFF_HEADS, DIFF_HD), F32),
                 jax.ShapeDtypeStruct((n * DIFF_HEADS, DIFF_HD), F32)]
    if sample:
        in_specs.append(const(lp["wabs"]))
        args.append(lp["wabs"])
        out_specs += [pl.BlockSpec((MLA_HEADS, tm, 2 * LANE), lambda i: (0, i, 0)), row(DIFF_W)]
        out_shape += [jax.ShapeDtypeStruct((MLA_HEADS, n, 2 * LANE), F32), jax.ShapeDtypeStruct((n, DIFF_W), F32)]
    else:
        in_specs.append(const(lp["wkv"]))
        args.append(lp["wkv"])
        out_specs += [heads(MLA_HEADS)] * 3 + [pl.BlockSpec((DIFF_HEADS, 2, tm, LANE), lambda i: (0, 0, i, 0))]
        out_specs += [heads(DIFF_HEADS)] * 2
        out_shape += [jax.ShapeDtypeStruct((MLA_HEADS, n, LANE), BF16)] * 3
        out_shape += [jax.ShapeDtypeStruct((DIFF_HEADS, 2, n, LANE), BF16)]
        out_shape += [jax.ShapeDtypeStruct((DIFF_HEADS, n, LANE), BF16)] * 2
    weights = 2 * 2 * (D_MODEL * IN_COLS_P + 2 * Q_LORA * 2 * MLA_W + DIFF_W * DIFF_W)
    tiles = tm * (IN_COLS_P * 4 * 2 + 4 * MLA_W * 4 + 8 * DIFF_W * 4 + 2 * (D_MODEL + 6 * LANE) * 4
                  + 2 * (MLA_ROW + 2 * DIFF_W) * 4 + 2 * (3 * MLA_W + 4 * DIFF_W) * 4)
    return pl.pallas_call(
        functools.partial(_proj_kernel, sample=sample),
        grid=(n // tm,),
        in_specs=in_specs,
        out_specs=out_specs,
        out_shape=out_shape,
        compiler_params=pltpu.CompilerParams(
            dimension_semantics=("arbitrary",), vmem_limit_bytes=_vmem_limit(weights + tiles)),
        name="proj_sample" if sample else "proj_prompt",
    )(*args)


def _diff_finish(o1, o2, par):
    d = o1 - par[0:1, :] * o2
    return _rms(d, DIFF_HD) * par[1:2, :]


def _flash_kernel(*refs, nmaps, heads, blk):
    if nmaps == 2:
        q_ref, k_ref, v_ref, par_ref, o_ref = refs
    else:
        q_ref, k_ref, v_ref, o_ref = refs
    i = pl.program_id(2)
    m_rows = nmaps * blk

    def step(j, carries, diagonal):
        start = pl.multiple_of(j * blk, blk)
        out = []
        for hh in range(heads):
            m, l, acc = carries[hh]
            s = _dot_nt(q_ref[hh].reshape(m_rows, LANE), k_ref[hh, pl.ds(start, blk), :])
            if diagonal:
                r = lax.broadcasted_iota(jnp.int32, (m_rows, blk), 0)
                c = lax.broadcasted_iota(jnp.int32, (m_rows, blk), 1)
                if nmaps == 2:
                    r = jnp.where(r >= blk, r - blk, r)
                s = jnp.where(c <= r, s, NEG)
            m_new = jnp.maximum(m, jnp.max(s, axis=-1, keepdims=True))
            p = jnp.exp2(s - m_new)
            corr = jnp.exp2(m - m_new)
            if nmaps == 2:
                l = l * corr + jnp.sum(p, axis=-1, keepdims=True)
            acc = acc * corr + _dot(p.astype(BF16), v_ref[hh, pl.ds(start, blk), :])
            out.append((m_new, l, acc))
        return tuple(out)

    init = tuple((jnp.full((m_rows, 1), NEG, F32), jnp.zeros((m_rows, 1), F32), jnp.zeros((m_rows, LANE), F32))
                 for _ in range(heads))
    carries = lax.fori_loop(0, i, lambda j, c: step(j, c, False), init)
    fin = step(i, carries, True)
    for hh in range(heads):
        _, l, acc = fin[hh]
        if nmaps == 2:
            o = acc / l
            o = _diff_finish(o[:blk], o[blk:], par_ref[...])
        else:
            o = acc / acc[:, MLA_V:MLA_V + 1]
        o_ref[:, hh * LANE:(hh + 1) * LANE] = o.astype(o_ref.dtype)


def _flash_call(q, k, v, batch, seq, heads, par=None):
    nmaps = 2 if q.ndim == 4 else 1
    units = q.shape[0]
    blk = min(ATTN_TILE, seq)
    assert seq % blk == 0 and units % heads == 0
    nq = seq // blk
    if nmaps == 2:
        q_spec = pl.BlockSpec((heads, 2, blk, LANE), lambda b, u, i: (u, 0, b * nq + i, 0))
    else:
        q_spec = pl.BlockSpec((heads, blk, LANE), lambda b, u, i: (u, b * nq + i, 0))
    kv_spec = pl.BlockSpec((heads, seq, LANE), lambda b, u, i: (u, b, 0))
    in_specs = [q_spec, kv_spec, kv_spec]
    args = [q, k, v]
    if nmaps == 2:
        in_specs.append(pl.BlockSpec(par.shape, lambda b, u, i: (0, 0)))
        args.append(par)
    need = 2 * 2 * heads * seq * LANE * 2 + heads * nmaps * blk * (7 * blk * 4 + 8 * LANE * 4)
    return pl.pallas_call(
        functools.partial(_flash_kernel, nmaps=nmaps, heads=heads, blk=blk),
        grid=(batch, units // heads, nq),
        in_specs=in_specs,
        out_specs=pl.BlockSpec((blk, heads * LANE), lambda b, u, i: (b * nq + i, u)),
        out_shape=jax.ShapeDtypeStruct((batch * seq, units * LANE), BF16),
        compiler_params=pltpu.CompilerParams(
            dimension_semantics=("arbitrary", "arbitrary", "arbitrary"), vmem_limit_bytes=_vmem_limit(need)),
        name="flash_diff" if nmaps == 2 else "flash_mla",
    )(*args)


def _decode_kernel(pt_ref, cm_hbm, ck_hbm, cv_hbm, qa_ref, qd_ref, rown_ref, kn_ref, vn_ref, wuk_ref,
                   om_ref, od_ref,
                   mbuf, kbuf, vbuf, sems, mm_ref, lm_ref, am_ref, md_ref, ld_ref, ad_ref,
                   *, page, pages_per_step, page_base, sample_len):
    b = pl.program_id(0)
    c = pl.program_id(1)
    nchunk = pl.num_programs(1)
    step = b * nchunk + c
    slot = step % 2
    tok = page * pages_per_step
    ncols = MLA_HEADS * sample_len

    def copies(bb, cc, sl):
        out = []
        for j in range(pages_per_step):
            pg = pt_ref[bb, cc * pages_per_step + j] + page_base
            out.append(pltpu.make_async_copy(
                cm_hbm.at[pl.ds(pl.multiple_of(pg * MLA_ROW, MLA_ROW), MLA_ROW)],
                mbuf.at[sl, :, pl.ds(j * page, page)], sems.at[sl, 0]))
            rows = pl.ds(pl.multiple_of(pg * (page * DIFF_HEADS), page * DIFF_HEADS), page * DIFF_HEADS)
            dst = pl.ds(j * page * DIFF_HEADS, page * DIFF_HEADS)
            out.append(pltpu.make_async_copy(ck_hbm.at[rows], kbuf.at[sl, dst], sems.at[sl, 1]))
            out.append(pltpu.make_async_copy(cv_hbm.at[rows], vbuf.at[sl, dst], sems.at[sl, 2]))
        return out

    @pl.when(step == 0)
    def _():
        for cp in copies(0, 0, 0):
            cp.start()

    @pl.when(step + 1 < pl.num_programs(0) * nchunk)
    def _():
        nxt = step + 1
        for cp in copies(nxt // nchunk, nxt % nchunk, 1 - slot):
            cp.start()

    @pl.when(c == 0)
    def _():
        mm_ref[...] = jnp.full(mm_ref.shape, NEG, F32)
        md_ref[...] = jnp.full(md_ref.shape, NEG, F32)
        lm_ref[...] = jnp.zeros(lm_ref.shape, F32)
        ld_ref[...] = jnp.zeros(ld_ref.shape, F32)
        am_ref[...] = jnp.zeros(am_ref.shape, F32)
        ad_ref[...] = jnp.zeros(ad_ref.shape, F32)

    for cp in copies(b, c, slot):
        cp.wait()

    def online(s, pv_fn, m_ref, l_ref, a_ref):
        m_old = m_ref[...]
        m_new = jnp.maximum(m_old, jnp.max(s, axis=1, keepdims=True))
        p = jnp.exp2(s - m_new)
        corr = jnp.exp2(m_old - m_new)
        l_ref[...] = l_ref[...] * corr + jnp.sum(p, axis=1, keepdims=True)
        a_ref[...] = a_ref[...] * corr + pv_fn(p.astype(BF16))
        m_ref[...] = m_new

    def mla_scores(row_t):
        n = row_t.shape[1]
        rowb = row_t.astype(BF16)
        lhs = jnp.concatenate([wuk_ref[...], qa_ref[0]], axis=0)
        y = _dot(lhs, rowb)
        kn = y[:MLA_HEADS * MLA_NOPE]
        ss = jnp.sum((kn * kn).reshape(MLA_NOPE, MLA_HEADS, n), axis=0)
        kr = row_t[KV_LORA:MLA_ROW]
        ss = ss + jnp.sum(kr * kr, axis=0, keepdims=True)
        rs = lax.rsqrt(ss * (1.0 / MLA_QK) + EPS)
        s = y[MLA_HEADS * MLA_NOPE:] * jnp.concatenate([rs] * sample_len, axis=0)
        return s, rowb[:KV_LORA]

    def heads_of(ref_fn):
        return jnp.concatenate([ref_fn(hh) for hh in range(DIFF_HEADS)], axis=-1).astype(BF16)

    kc = heads_of(lambda hh: kbuf[slot, pl.ds(hh, tok, stride=DIFF_HEADS), :])
    s_d = _dot_nt(qd_ref[0], kc)
    s_m, cb = mla_scores(mbuf[slot])
    vc = heads_of(lambda hh: vbuf[slot, pl.ds(hh, tok, stride=DIFF_HEADS), :])
    online(s_d, lambda p: _dot(p, vc), md_ref, ld_ref, ad_ref)
    online(s_m, lambda p: _dot_nt(p, cb), mm_ref, lm_ref, am_ref)

    @pl.when(c == nchunk - 1)
    def _():
        def mask(s, t_of_row):
            col = lax.broadcasted_iota(jnp.int32, s.shape, 1)
            row = lax.broadcasted_iota(jnp.int32, s.shape, 0)
            return jnp.where(col <= t_of_row(row), s, NEG)

        sn_m, cnb = mla_scores(rown_ref[0])
        online(mask(sn_m, lambda r: r // MLA_HEADS), lambda p: _dot_nt(p, cnb), mm_ref, lm_ref, am_ref)
        knb = kn_ref[0].astype(BF16)
        vnb = vn_ref[0].astype(BF16)
        sn_d = _dot_nt(qd_ref[0], knb)
        online(mask(sn_d, lambda r: r % sample_len), lambda p: _dot(p, vnb), md_ref, ld_ref, ad_ref)
        om_ref[0] = am_ref[...] / lm_ref[...]
        l_d = ld_ref[...]
        rows_per_head = 2 * sample_len
        for hh in range(DIFF_HEADS):
            rsl = slice(hh * rows_per_head, (hh + 1) * rows_per_head)
            od_ref[0, rsl, :] = ad_ref[rsl, hh * LANE:(hh + 1) * LANE] / l_d[rsl]


def _decode_call(page_table, cm_rows, ck_rows, cv_rows, qa, qd, row_new, k_new, v_new, wuk, page, page_base):
    bsz, n_pages = page_table.shape
    pps = min(DECODE_PAGES, n_pages)
    assert n_pages % pps == 0
    nchunk = n_pages // pps
    ncols = qa.shape[1]
    sample_len = ncols // MLA_HEADS
    tok = page * pps
    per_b = lambda shape: pl.BlockSpec((1,) + shape, lambda b, c, pt: (b, 0, 0))
    grid_spec = pltpu.PrefetchScalarGridSpec(
        num_scalar_prefetch=1,
        grid=(bsz, nchunk),
        in_specs=[pl.BlockSpec(memory_space=pl.ANY)] * 3 + [
            per_b((ncols, MLA_ROW)), per_b((ncols, DIFF_W)),
            per_b((MLA_ROW, page)), per_b((NEW_PAD, DIFF_W)), per_b((NEW_PAD, DIFF_W)),
            pl.BlockSpec(wuk.shape, lambda b, c, pt: (0, 0))],
        out_specs=[per_b((ncols, LANE)), per_b((ncols, LANE))],
        scratch_shapes=[
            pltpu.VMEM((2, MLA_ROW, tok), F32),
            pltpu.VMEM((2, tok * DIFF_HEADS, LANE), F32),
            pltpu.VMEM((2, tok * DIFF_HEADS, LANE), F32),
            pltpu.SemaphoreType.DMA((2, 3)),
            pltpu.VMEM((ncols, 1), F32), pltpu.VMEM((ncols, 1), F32), pltpu.VMEM((ncols, KV_LORA), F32),
            pltpu.VMEM((ncols, 1), F32), pltpu.VMEM((ncols, 1), F32), pltpu.VMEM((ncols, DIFF_W), F32),
        ])
    need = 2 * tok * (MLA_ROW + 2 * DIFF_W) * 4 + tok * (2 * (DIFF_W + MLA_ROW) * 2 + (MLA_HEADS * MLA_NOPE + 4 * ncols) * 4 * 2)
    return pl.pallas_call(
        functools.partial(_decode_kernel, page=page, pages_per_step=pps, page_base=page_base,
                          sample_len=sample_len),
        grid_spec=grid_spec,
        out_shape=[jax.ShapeDtypeStruct((bsz, ncols, LANE), F32)] * 2,
        compiler_params=pltpu.CompilerParams(
            dimension_semantics=("arbitrary", "arbitrary"), vmem_limit_bytes=_vmem_limit(need)),
        name="decode",
    )(page_table, cm_rows, ck_rows, cv_rows, qa, qd, row_new, k_new, v_new, wuk)


def _sample_combine_kernel(om_ref, od_ref, wuv_ref, par_ref, cm_ref, cd_ref):
    for hh in range(MLA_HEADS):
        cm_ref[:, hh * LANE:(hh + 1) * LANE] = _dot(om_ref[hh].astype(BF16), wuv_ref[hh]).astype(BF16)
    for hh in range(DIFF_HEADS):
        cd_ref[:, hh * LANE:(hh + 1) * LANE] = _diff_finish(od_ref[hh, 0], od_ref[hh, 1], par_ref[...]).astype(BF16)


def _sample_combine_call(om, od, wuv, par):
    n = om.shape[1]
    full = lambda a: pl.BlockSpec(a.shape, lambda i: (0,) * a.ndim)
    return pl.pallas_call(
        _sample_combine_kernel,
        grid=(1,),
        in_specs=[full(om), full(od), full(wuv), full(par)],
        out_specs=[pl.BlockSpec((n, MLA_W), lambda i: (0, 0)), pl.BlockSpec((n, DIFF_W), lambda i: (0, 0))],
        out_shape=[jax.ShapeDtypeStruct((n, MLA_W), BF16), jax.ShapeDtypeStruct((n, DIFF_W), BF16)],
        name="sample_combine",
    )(om, od, wuv, par)


def _rot_half_cols(w, group, half):
    w3 = w.reshape(w.shape[0], -1, group)
    zeros = jnp.zeros_like(w3[..., 2 * half:])
    return jnp.concatenate([-w3[..., half:2 * half], w3[..., :half], zeros], axis=-1).reshape(w.shape)


def _rot_half_gain(g, half):
    return jnp.concatenate([g[half:2 * half], g[:half], jnp.zeros_like(g[2 * half:])])


def _prep_layer(w_in, mix_norm, mla_q_norm, mla_kv_norm, w_uq, w_uk, w_uv, mla_q_gain, mla_k_gain,
                diff_q_gain, diff_k_gain):
    o1 = Q_LORA
    o2 = o1 + KV_LORA
    o3 = o2 + MLA_ROPE
    o4 = o3 + DIFF_W
    o5 = o4 + DIFF_W
    wkr = w_in[:, o2:o3]
    wdq, wdk, wdv = w_in[:, o3:o4], w_in[:, o4:o5], w_in[:, o5:]
    pad = jnp.zeros((D_MODEL, LANE - MLA_ROPE), F32)
    w_in_p = jnp.concatenate(
        [w_in[:, :o1], w_in[:, o1:o2], wdq, wdk, wdv,
         _rot_half_cols(wdq, DIFF_D, DIFF_ROT // 2), _rot_half_cols(wdk, DIFF_D, DIFF_ROT // 2),
         wkr, pad, _rot_half_cols(wkr, MLA_ROPE, MLA_ROPE // 2), pad], axis=1).astype(BF16)
    assert w_in_p.shape[1] == IN_COLS_P

    hpad = jnp.zeros((Q_LORA, MLA_HEADS, LANE - MLA_QK), F32)
    wq = jnp.concatenate([w_uq, hpad], axis=-1).reshape(Q_LORA, MLA_W)
    wq_rope = w_uq[..., MLA_NOPE:]
    wq_sw = jnp.concatenate(
        [jnp.zeros((Q_LORA, MLA_HEADS, MLA_NOPE), F32),
         _rot_half_cols(wq_rope.reshape(Q_LORA, -1), MLA_ROPE, MLA_ROPE // 2).reshape(Q_LORA, MLA_HEADS, MLA_ROPE),
         hpad], axis=-1).reshape(Q_LORA, MLA_W)
    wq2 = jnp.concatenate([wq, wq_sw], axis=1).astype(BF16)

    vpad = jnp.zeros((KV_LORA, MLA_HEADS, LANE - MLA_NOPE), F32)
    wk_c = jnp.concatenate([w_uk, vpad], axis=-1).reshape(KV_LORA, MLA_W)
    place = jnp.concatenate([jnp.zeros((MLA_ROPE, MLA_NOPE), F32), jnp.eye(MLA_ROPE, dtype=F32),
                             jnp.zeros((MLA_ROPE, LANE - MLA_QK), F32)], axis=1)
    wk_r = jnp.tile(place, (1, MLA_HEADS))
    wk = jnp.concatenate([wk_c, wk_r, jnp.zeros((LANE - MLA_ROPE, MLA_W), F32)], axis=0)
    wv = jnp.concatenate([jnp.concatenate([w_uv, vpad], axis=-1).reshape(KV_LORA, MLA_W),
                          jnp.zeros((LANE, MLA_W), F32)], axis=0)
    wkv = jnp.concatenate([wk, wv], axis=1).astype(BF16)

    wabs_top = jnp.concatenate([jnp.transpose(w_uk, (1, 2, 0)), jnp.zeros((MLA_HEADS, MLA_NOPE, LANE), F32)], axis=-1)
    wabs_mid = jnp.concatenate([jnp.zeros((MLA_ROPE, LANE), F32), jnp.eye(MLA_ROPE, dtype=F32),
                                jnp.zeros((MLA_ROPE, LANE - MLA_ROPE), F32)], axis=1)
    wabs = jnp.concatenate([wabs_top, jnp.broadcast_to(wabs_mid, (MLA_HEADS,) + wabs_mid.shape),
                            jnp.zeros((MLA_HEADS, LANE - MLA_QK, 2 * LANE), F32)], axis=1).astype(BF16)

    wuk_t = jnp.concatenate([jnp.transpose(w_uk, (2, 1, 0)).reshape(MLA_HEADS * MLA_NOPE, KV_LORA),
                             jnp.zeros((MLA_HEADS * MLA_NOPE, MLA_ROPE), F32)], axis=1).astype(BF16)
    wuv_pad = jnp.concatenate([jnp.transpose(w_uv, (1, 0, 2)), jnp.zeros((MLA_HEADS, KV_LORA, LANE - MLA_V), F32)],
                              axis=-1).astype(BF16)

    qpad = jnp.zeros((LANE - MLA_QK,), F32)
    row1 = jnp.concatenate([mla_q_norm, mla_kv_norm, mla_q_gain, qpad, mla_k_gain, qpad,
                            jnp.zeros((D_MODEL - Q_LORA - KV_LORA - 2 * LANE,), F32)])
    reps = DIFF_W // DIFF_D
    row2 = jnp.concatenate([jnp.tile(diff_q_gain, reps), jnp.tile(_rot_half_gain(diff_q_gain, DIFF_ROT // 2), reps)])
    row3 = jnp.concatenate([jnp.tile(diff_k_gain, reps), jnp.tile(_rot_half_gain(diff_k_gain, DIFF_ROT // 2), reps)])
    vec = jnp.stack([mix_norm, row1, row2, row3] + [jnp.zeros((D_MODEL,), F32)] * 4)
    b64 = jnp.kron(jnp.eye(DIFF_W // DIFF_D, dtype=F32), jnp.ones((DIFF_D, DIFF_D), F32)).astype(BF16)
    return dict(w_in=w_in_p, wq=wq2, wkv=wkv, wabs=wabs, wuk_t=wuk_t, wuv_pad=wuv_pad, vec=vec, b64=b64)


def _rope_tables(pos):
    n = pos.shape[0]

    def cs(r):
        half = r // 2
        inv = jnp.power(jnp.float32(ROPE_THETA), -jnp.arange(half, dtype=F32) * (2.0 / r))
        ang = pos.astype(F32)[:, None] * inv[None, :]
        return jnp.cos(ang), jnp.sin(ang)

    cm, sm = cs(MLA_ROPE)
    cd, sd = cs(DIFF_ROT)
    ones = lambda w: jnp.ones((n, w), F32)
    zeros = lambda w: jnp.zeros((n, w), F32)
    c64 = jnp.concatenate([cd, cd, ones(DIFF_D - DIFF_ROT)], axis=1)
    s64 = jnp.concatenate([sd, sd, zeros(DIFF_D - DIFF_ROT)], axis=1)
    return jnp.concatenate([
        ones(MLA_NOPE), cm, cm, zeros(LANE - MLA_QK),
        zeros(MLA_NOPE), sm, sm, zeros(LANE - MLA_QK),
        cm, cm, zeros(LANE - MLA_ROPE),
        sm, sm, zeros(LANE - MLA_ROPE),
        c64, c64, s64, s64], axis=1)


def kernel(x_prompt, x_sample, cache_mla, cache_diff_k, cache_diff_v, page_table, ffn1_norm, ffn1_w_gate, ffn1_w_up, ffn1_w_down, mix_norm, w_in, mla_q_norm, mla_kv_norm, w_uq, w_uk, w_uv, mla_q_gain, mla_k_gain, diff_q_gain, diff_k_gain, diff_lq1, diff_lk1, diff_lq2, diff_lk2, diff_sub_gain, w_o, ffn2_norm, ffn2_w_gate, ffn2_w_up, ffn2_w_down):
    depth = w_in.shape[0]
    bp, seq, _ = x_prompt.shape
    bs, ts, _ = x_sample.shape
    n_pool, page = cache_mla.shape[1], cache_mla.shape[2]
    past_len = page_table.shape[1] * page
    assert ts <= NEW_PAD and MLA_HEADS * ts == 2 * DIFF_HEADS * ts

    cm_rows = jnp.swapaxes(cache_mla, 2, 3).reshape(-1, page)
    ck_rows = cache_diff_k.reshape(-1, DIFF_HD)
    cv_rows = cache_diff_v.reshape(-1, DIFF_HD)

    tab_p = _rope_tables(jnp.arange(seq, dtype=jnp.int32))
    tab_s = jnp.tile(_rope_tables(past_len + jnp.arange(ts, dtype=jnp.int32)), (bs, 1))

    yp = x_prompt.reshape(bp * seq, D_MODEL)
    ys = x_sample.reshape(bs * ts, D_MODEL)
    outs = [[] for _ in range(6)]
    for l in range(depth):
        lp = _prep_layer(w_in[l], mix_norm[l], mla_q_norm[l], mla_kv_norm[l], w_uq[l], w_uk[l], w_uv[l],
                         mla_q_gain[l], mla_k_gain[l], diff_q_gain[l], diff_k_gain[l])
        lam_init = 0.8 - 0.6 * math.exp(-0.3 * l)
        lam = (jnp.exp(jnp.sum(diff_lq1[l] * diff_lk1[l])) - jnp.exp(jnp.sum(diff_lq2[l] * diff_lk2[l])) + lam_init)
        par = jnp.stack([jnp.broadcast_to(lam, (DIFF_HD,)), diff_sub_gain[l] * (1.0 - lam_init)]
                        + [jnp.zeros((DIFF_HD,), F32)] * 6)
        f1 = (ffn1_w_gate[l].astype(BF16), ffn1_w_up[l].astype(BF16), ffn1_w_down[l].astype(BF16))
        f2 = (ffn2_w_gate[l].astype(BF16), ffn2_w_up[l].astype(BF16), ffn2_w_down[l].astype(BF16))
        wo_m = jnp.concatenate([w_o[l, :MLA_HEADS * MLA_V].reshape(MLA_HEADS, MLA_V, D_MODEL),
                                jnp.zeros((MLA_HEADS, LANE - MLA_V, D_MODEL), F32)], axis=1
                               ).reshape(MLA_W, D_MODEL).astype(BF16)
        wo_d = w_o[l, MLA_HEADS * MLA_V:].astype(BF16)

        x1p = _ffn_call(yp, ffn1_norm[l], *f1)
        row_p, dk_p, dv_p, qm, km, vm, qd, kd, vd = _proj_call(x1p, tab_p, lp, sample=False)
        cat_m = _flash_call(qm, km, vm, bp, seq, MLA_HEADS_PER_STEP)
        cat_d = _flash_call(qd, kd, vd, bp, seq, 1, par=par)
        yp = _ffn_call(x1p, ffn2_norm[l], *f2, attn=(cat_m, cat_d, wo_m, wo_d))

        x1s = _ffn_call(ys, ffn1_norm[l], *f1)
        row_s, dk_s, dv_s, qabs, dq_s = _proj_call(x1s, tab_s, lp, sample=True)
        qa = jnp.transpose(qabs[:, :, :MLA_ROW].reshape(MLA_HEADS, bs, ts, MLA_ROW), (1, 2, 0, 3))
        qa = qa.reshape(bs, ts * MLA_HEADS, MLA_ROW).astype(BF16)
        dq5 = dq_s.reshape(bs, ts, DIFF_HEADS, 2, DIFF_D)
        qbd = jnp.einsum("bthmd,hH,mM->bhmtHMd", dq5, jnp.eye(DIFF_HEADS, dtype=F32), jnp.eye(2, dtype=F32))
        qbd = qbd.reshape(bs, DIFF_HEADS * 2 * ts, DIFF_W).astype(BF16)
        padn = lambda a: jnp.pad(a.reshape(bs, ts, -1), ((0, 0), (0, NEW_PAD - ts), (0, 0)))
        row_new = jnp.pad(jnp.swapaxes(row_s.reshape(bs, ts, MLA_ROW), 1, 2), ((0, 0), (0, 0), (0, page - ts)))
        om, od = _decode_call(page_table, cm_rows, ck_rows, cv_rows, qa, qbd, row_new, padn(dk_s), padn(dv_s),
                              lp["wuk_t"], page, l * n_pool)
        om = jnp.transpose(om.reshape(bs, ts, MLA_HEADS, KV_LORA), (2, 0, 1, 3)).reshape(MLA_HEADS, bs * ts, KV_LORA)
        od = jnp.transpose(od.reshape(bs, DIFF_HEADS, 2, ts, DIFF_HD), (1, 2, 0, 3, 4)
                           ).reshape(DIFF_HEADS, 2, bs * ts, DIFF_HD)
        cat_ms, cat_ds = _sample_combine_call(om, od, lp["wuv_pad"], par)
        ys = _ffn_call(x1s, ffn2_norm[l], *f2, attn=(cat_ms, cat_ds, wo_m, wo_d))

        outs[0].append(row_p.reshape(bp, seq, MLA_ROW))
        outs[1].append(dk_p.reshape(bp, seq, DIFF_HEADS, DIFF_HD))
        outs[2].append(dv_p.reshape(bp, seq, DIFF_HEADS, DIFF_HD))
        outs[3].append(row_s.reshape(bs, ts, MLA_ROW))
        outs[4].append(dk_s.reshape(bs, ts, DIFF_HEADS, DIFF_HD))
        outs[5].append(dv_s.reshape(bs, ts, DIFF_HEADS, DIFF_HD))
    return (yp.reshape(bp, seq, D_MODEL), ys.reshape(bs, ts, D_MODEL)) + tuple(jnp.stack(o) for o in outs)
```

```python
import functools
import math

import jax
import jax.numpy as jnp
from jax import lax
from jax.experimental import pallas as pl
from jax.experimental.pallas import tpu as pltpu

F32 = jnp.float32
BF16 = jnp.bfloat16

D_MODEL = 1024
MLA_HEADS = 8
MLA_V = 64
MLA_NOPE = 64
MLA_ROPE = 32
MLA_QK = MLA_NOPE + MLA_ROPE
Q_LORA = 256
KV_LORA = 128
MLA_ROW = KV_LORA + MLA_ROPE
DIFF_D = 64
DIFF_HD = 2 * DIFF_D
DIFF_HEADS = 4
DIFF_ROT = DIFF_D // 4
D_FF = 2816
ROPE_THETA = 500000.0
EPS = 1e-6
NEG = -1e30
LOG2E = 1.4426950408889634

LANE = 128
V7X_VMEM_BYTES = 64 * 1024 * 1024

TOKEN_TILE = 512
ATTN_TILE = 1024
MLA_HEADS_PER_STEP = 2
DECODE_PAGES = 16
NEW_PAD = 16

_C_CQ = 0
_C_CKV = _C_CQ + Q_LORA
_C_DQ = _C_CKV + KV_LORA
_C_DK = _C_DQ + DIFF_HEADS * DIFF_HD
_C_DV = _C_DK + DIFF_HEADS * DIFF_HD
_C_DQS = _C_DV + DIFF_HEADS * DIFF_HD
_C_DKS = _C_DQS + DIFF_HEADS * DIFF_HD
_C_KRA = _C_DKS + DIFF_HEADS * DIFF_HD
_C_KRB = _C_KRA + LANE
IN_COLS_P = _C_KRB + LANE
DIFF_W = DIFF_HEADS * DIFF_HD
MLA_W = MLA_HEADS * LANE


def _vmem_limit(nbytes):
    return int(min(max(nbytes, 32 * 1024 * 1024), V7X_VMEM_BYTES - 8 * 1024 * 1024))


def _rms(x, n):
    return x * lax.rsqrt(jnp.sum(x * x, axis=-1, keepdims=True) * (1.0 / n) + EPS)


def _dot(a, b):
    return jnp.dot(a, b, preferred_element_type=F32)


def _dot_nt(a, b):
    return lax.dot_general(a, b, (((1,), (1,)), ((), ())), preferred_element_type=F32)


def _ffn_kernel(*refs, with_attn):
    if with_attn:
        x_ref, cm_ref, cd_ref, wom_ref, wod_ref, g_ref, wg_ref, wu_ref, wd_ref, o_ref = refs
        x = x_ref[...] + _dot(cm_ref[...], wom_ref[...]) + _dot(cd_ref[...], wod_ref[...])
    else:
        x_ref, g_ref, wg_ref, wu_ref, wd_ref, o_ref = refs
        x = x_ref[...]
    h = (_rms(x, D_MODEL) * g_ref[...]).astype(BF16)
    g = _dot(h, wg_ref[...])
    u = _dot(h, wu_ref[...])
    a = (g * (1.0 / (1.0 + jnp.exp(-g))) * u).astype(BF16)
    o_ref[...] = x + 0.5 * _dot(a, wd_ref[...])


def _const_spec(shape):
    return pl.BlockSpec(shape, lambda i: (0,) * len(shape), pipeline_mode=pl.Buffered(1))


def _ffn_call(x, norm_g, wg, wu, wd, attn=None):
    n = x.shape[0]
    tm = min(TOKEN_TILE, n)
    assert n % tm == 0
    row = lambda w: pl.BlockSpec((tm, w), lambda i: (i, 0))
    in_specs = [row(D_MODEL)]
    args = [x]
    if attn is not None:
        cm, cd, wom, wod = attn
        in_specs += [row(MLA_W), row(DIFF_W), _const_spec(wom.shape), _const_spec(wod.shape)]
        args += [cm, cd, wom, wod]
    in_specs += [_const_spec((1, D_MODEL)), _const_spec(wg.shape), _const_spec(wu.shape), _const_spec(wd.shape)]
    args += [norm_g.reshape(1, D_MODEL), wg, wu, wd]
    weights = 3 * D_MODEL * D_FF * 2 + (MLA_W + DIFF_W) * D_MODEL * 2
    tiles = tm * (4 * D_MODEL * 4 + 2 * D_FF * 4 + D_FF * 2 + 3 * D_MODEL * 4 + 2 * (MLA_W + DIFF_W) * 2)
    return pl.pallas_call(
        functools.partial(_ffn_kernel, with_attn=attn is not None),
        grid=(n // tm,),
        in_specs=in_specs,
        out_specs=row(D_MODEL),
        out_shape=jax.ShapeDtypeStruct((n, D_MODEL), F32),
        compiler_params=pltpu.CompilerParams(
            dimension_semantics=("arbitrary",), vmem_limit_bytes=_vmem_limit(weights + tiles)),
        name="ffn_attn" if attn is not None else "ffn",
    )(*args)


def _proj_kernel(*refs, sample):
    if sample:
        (x_ref, tab_ref, vec_ref, win_ref, wq_ref, b64_ref, wabs_ref,
         row_ref, dk_ref, dv_ref, qa_ref, dq_ref) = refs
    else:
        (x_ref, tab_ref, vec_ref, win_ref, wq_ref, b64_ref, wkv_ref,
         row_ref, dk_ref, dv_ref, qm_ref, km_ref, vm_ref, qd_ref, kd_ref, vd_ref) = refs
    x = x_ref[...]
    h = (_rms(x, D_MODEL) * vec_ref[0:1, :]).astype(BF16)
    z = _dot(h, win_ref[...])
    tab = tab_ref[...]
    cq_t, sq_t, ck_t, sk_t, cd_t, sd_t = (tab[:, i * LANE:(i + 1) * LANE] for i in range(6))
    gq = vec_ref[1:2, 384:512]
    gk = vec_ref[1:2, 512:640]

    ckv = _rms(z[:, _C_CKV:_C_CKV + KV_LORA], KV_LORA) * vec_ref[1:2, 256:384]
    krt = z[:, _C_KRA:_C_KRA + LANE] * ck_t + z[:, _C_KRB:_C_KRB + LANE] * sk_t
    row256 = jnp.concatenate([ckv, krt], axis=-1)
    row_ref[...] = row256[:, :MLA_ROW]

    c512 = jnp.concatenate([cd_t] * (DIFF_W // LANE), axis=-1)
    s512 = jnp.concatenate([sd_t] * (DIFF_W // LANE), axis=-1)

    def diff_norm(a, a_sw, g, g_sw):
        ss = _dot((a * a).astype(BF16), b64_ref[...])
        return lax.rsqrt(ss * (1.0 / DIFF_D) + EPS) * (a * g * c512 + a_sw * g_sw * s512)

    dq = diff_norm(z[:, _C_DQ:_C_DQ + DIFF_W], z[:, _C_DQS:_C_DQS + DIFF_W],
                   vec_ref[2:3, 0:DIFF_W], vec_ref[2:3, DIFF_W:2 * DIFF_W])
    dk = diff_norm(z[:, _C_DK:_C_DK + DIFF_W], z[:, _C_DKS:_C_DKS + DIFF_W],
                   vec_ref[3:4, 0:DIFF_W], vec_ref[3:4, DIFF_W:2 * DIFF_W])
    dv = z[:, _C_DV:_C_DV + DIFF_W]
    tm = x.shape[0]
    for hh in range(DIFF_HEADS):
        dk_ref[pl.ds(hh, tm, stride=DIFF_HEADS), :] = dk[:, hh * LANE:(hh + 1) * LANE]
        dv_ref[pl.ds(hh, tm, stride=DIFF_HEADS), :] = dv[:, hh * LANE:(hh + 1) * LANE]
    dq = dq * (DIFF_D ** -0.5 * LOG2E)
    lane = lax.broadcasted_iota(jnp.int32, (tm, LANE), 1)

    cqn = (_rms(z[:, _C_CQ:_C_CQ + Q_LORA], Q_LORA) * vec_ref[1:2, 0:Q_LORA]).astype(BF16)
    q2 = _dot(cqn, wq_ref[...])
    if not sample:
        kv = _dot(row256.astype(BF16), wkv_ref[...])
    for hh in range(MLA_HEADS):
        sl = slice(hh * LANE, (hh + 1) * LANE)
        qh = q2[:, sl] * cq_t + q2[:, MLA_W + hh * LANE:MLA_W + (hh + 1) * LANE] * sq_t
        qn = _rms(qh, MLA_QK) * gq * (MLA_QK ** -0.5 * LOG2E)
        if sample:
            qa_ref[hh] = _dot((qn * gk).astype(BF16), wabs_ref[hh])
        else:
            qm_ref[hh] = qn.astype(BF16)
            km_ref[hh] = (_rms(kv[:, sl], MLA_QK) * gk).astype(BF16)
            vh = kv[:, MLA_W + hh * LANE:MLA_W + (hh + 1) * LANE]
            vm_ref[hh] = jnp.where(lane == MLA_V, 1.0, vh).astype(BF16)
    if sample:
        dq_ref[...] = dq
    else:
        for hh in range(DIFF_HEADS):
            sl = slice(hh * LANE, (hh + 1) * LANE)
            qd_ref[hh, 0] = jnp.where(lane < DIFF_D, dq[:, sl], 0.0).astype(BF16)
            qd_ref[hh, 1] = jnp.where(lane >= DIFF_D, dq[:, sl], 0.0).astype(BF16)
            kd_ref[hh] = dk[:, sl].astype(BF16)
            vd_ref[hh] = dv[:, sl].astype(BF16)


def _proj_call(x, tab, lp, sample):
    n = x.shape[0]
    tm = min(TOKEN_TILE, n)
    assert n % tm == 0 and tab.shape[0] % tm == 0
    tab_blocks = tab.shape[0] // tm
    row = lambda w: pl.BlockSpec((tm, w), lambda i: (i, 0))
    const = lambda a: pl.BlockSpec(a.shape, lambda i: (0,) * a.ndim)
    heads = lambda nh: pl.BlockSpec((nh, tm, LANE), lambda i: (0, i, 0))
    in_specs = [row(D_MODEL), pl.BlockSpec((tm, 6 * LANE), lambda i: (i % tab_blocks, 0)),
                const(lp["vec"]), const(lp["w_in"]), const(lp["wq"]), const(lp["b64"])]
    args = [x, tab, lp["vec"], lp["w_in"], lp["wq"], lp["b64"]]
    head_rows = pl.BlockSpec((tm * DIFF_HEADS, DIFF_HD), lambda i: (i, 0))
    out_specs = [row(MLA_ROW), head_rows, head_rows]
    out_shape = [jax.ShapeDtypeStruct((n, MLA_ROW), F32), jax.ShapeDtypeStruct((n * DIFF_HEADS, DIFF_HD), F32),
                 jax.ShapeDtypeStruct((n * DIFF_HEADS, DIFF_HD), F32)]
    if sample:
        in_specs.append(const(lp["wabs"]))
        args.append(lp["wabs"])
        out_specs += [pl.BlockSpec((MLA_HEADS, tm, 2 * LANE), lambda i: (0, i, 0)), row(DIFF_W)]
        out_shape += [jax.ShapeDtypeStruct((MLA_HEADS, n, 2 * LANE), F32), jax.ShapeDtypeStruct((n, DIFF_W), F32)]
    else:
        in_specs.append(const(lp["wkv"]))
        args.append(lp["wkv"])
        out_specs += [heads(MLA_HEADS)] * 3 + [pl.BlockSpec((DIFF_HEADS, 2, tm, LANE), lambda i: (0, 0, i, 0))]
        out_specs += [heads(DIFF_HEADS)] * 2
        out_shape += [jax.ShapeDtypeStruct((MLA_HEADS, n, LANE), BF16)] * 3
        out_shape += [jax.ShapeDtypeStruct((DIFF_HEADS, 2, n, LANE), BF16)]
        out_shape += [jax.ShapeDtypeStruct((DIFF_HEADS, n, LANE), BF16)] * 2
    weights = 2 * 2 * (D_MODEL * IN_COLS_P + 2 * Q_LORA * 2 * MLA_W + DIFF_W * DIFF_W)
    tiles = tm * (IN_COLS_P * 4 * 2 + 4 * MLA_W * 4 + 8 * DIFF_W * 4 + 2 * (D_MODEL + 6 * LANE) * 4
                  + 2 * (MLA_ROW + 2 * DIFF_W) * 4 + 2 * (3 * MLA_W + 4 * DIFF_W) * 4)
    return pl.pallas_call(
        functools.partial(_proj_kernel, sample=sample),
        grid=(n // tm,),
        in_specs=in_specs,
        out_specs=out_specs,
        out_shape=out_shape,
        compiler_params=pltpu.CompilerParams(
            dimension_semantics=("arbitrary",), vmem_limit_bytes=_vmem_limit(weights + tiles)),
        name="proj_sample" if sample else "proj_prompt",
    )(*args)


def _diff_finish(o1, o2, par):
    d = o1 - par[0:1, :] * o2
    return _rms(d, DIFF_HD) * par[1:2, :]


def _flash_kernel(q_ref, k_ref, v_ref, o_ref, *, heads, blk):
    i = pl.program_id(2)

    def step(j, carries, diagonal):
        start = pl.multiple_of(j * blk, blk)
        out = []
        for hh in range(heads):
            m, acc = carries[hh]
            s = _dot_nt(q_ref[hh], k_ref[hh, pl.ds(start, blk), :])
            if diagonal:
                r = lax.broadcasted_iota(jnp.int32, (blk, blk), 0)
                c = lax.broadcasted_iota(jnp.int32, (blk, blk), 1)
                s = jnp.where(c <= r, s, NEG)
            m_new = jnp.maximum(m, jnp.max(s, axis=-1, keepdims=True))
            p = jnp.exp2(s - m_new)
            acc = acc * jnp.exp2(m - m_new) + _dot(p.astype(BF16), v_ref[hh, pl.ds(start, blk), :])
            out.append((m_new, acc))
        return tuple(out)

    init = tuple((jnp.full((blk, 1), NEG, F32), jnp.zeros((blk, LANE), F32)) for _ in range(heads))
    carries = lax.fori_loop(0, i, lambda j, c: step(j, c, False), init)
    fin = step(i, carries, True)
    for hh in range(heads):
        acc = fin[hh][1]
        o = acc / acc[:, MLA_V:MLA_V + 1]
        o_ref[:, hh * LANE:(hh + 1) * LANE] = o.astype(o_ref.dtype)


def _flash_call(q, k, v, batch, seq, heads):
    units = q.shape[0]
    blk = min(ATTN_TILE, seq)
    assert seq % blk == 0 and units % heads == 0
    nq = seq // blk
    kv_spec = pl.BlockSpec((heads, seq, LANE), lambda b, u, i: (u, b, 0))
    need = 2 * 2 * heads * seq * LANE * 2 + heads * blk * (7 * blk * 4 + 8 * LANE * 4)
    return pl.pallas_call(
        functools.partial(_flash_kernel, heads=heads, blk=blk),
        grid=(batch, units // heads, nq),
        in_specs=[pl.BlockSpec((heads, blk, LANE), lambda b, u, i: (u, b * nq + i, 0)), kv_spec, kv_spec],
        out_specs=pl.BlockSpec((blk, heads * LANE), lambda b, u, i: (b * nq + i, u)),
        out_shape=jax.ShapeDtypeStruct((batch * seq, units * LANE), BF16),
        compiler_params=pltpu.CompilerParams(
            dimension_semantics=("arbitrary", "arbitrary", "arbitrary"), vmem_limit_bytes=_vmem_limit(need)),
        name="flash_mla",
    )(q, k, v)


def _flash_diff_kernel(q_ref, k_ref, v_ref, par_ref, o_ref, *, blk):
    i = pl.program_id(2)
    cols = 2 * blk

    def step(j, carry, diagonal):
        m, l, acc = carry
        start = pl.multiple_of(j * blk, blk)
        qt = jnp.concatenate([q_ref[0], q_ref[1]], axis=-1)
        s = _dot(k_ref[pl.ds(start, blk), :], qt)
        if diagonal:
            r = lax.broadcasted_iota(jnp.int32, (blk, cols), 0)
            c = lax.broadcasted_iota(jnp.int32, (blk, cols), 1)
            c = jnp.where(c >= blk, c - blk, c)
            s = jnp.where(r <= c, s, NEG)
        m_new = jnp.maximum(m, jnp.max(s, axis=0, keepdims=True))
        p = jnp.exp2(s - m_new)
        corr = jnp.exp2(m - m_new)
        l = l * corr + jnp.sum(p, axis=0, keepdims=True)
        acc = acc * corr + _dot(v_ref[:, pl.ds(start, blk)], p.astype(BF16))
        return m_new, l, acc

    init = (jnp.full((1, cols), NEG, F32), jnp.zeros((1, cols), F32), jnp.zeros((LANE, cols), F32))
    carry = lax.fori_loop(0, i, lambda j, c: step(j, c, False), init)
    _, l, acc = step(i, carry, True)
    o = acc / l
    d = o[:, :blk] - par_ref[:, 0:1] * o[:, blk:]
    d = d * lax.rsqrt(jnp.sum(d * d, axis=0, keepdims=True) * (1.0 / DIFF_HD) + EPS) * par_ref[:, 1:2]
    o_ref[...] = d.T.astype(o_ref.dtype)


def _flash_diff_call(q_t, k, v_t, par_t, batch, seq):
    units = q_t.shape[0]
    blk = min(ATTN_TILE, seq)
    assert seq % blk == 0
    nq = seq // blk
    need = 2 * 2 * seq * LANE * 2 + blk * (2 * blk * 14 + 16 * LANE * 4)
    return pl.pallas_call(
        functools.partial(_flash_diff_kernel, blk=blk),
        grid=(batch, units, nq),
        in_specs=[pl.BlockSpec((None, 2, LANE, blk), lambda b, u, i: (u, 0, 0, b * nq + i)),
                  pl.BlockSpec((None, seq, LANE), lambda b, u, i: (u, b, 0)),
                  pl.BlockSpec((None, LANE, seq), lambda b, u, i: (u, 0, b)),
                  pl.BlockSpec(par_t.shape, lambda b, u, i: (0, 0))],
        out_specs=pl.BlockSpec((blk, LANE), lambda b, u, i: (b * nq + i, u)),
        out_shape=jax.ShapeDtypeStruct((batch * seq, units * LANE), BF16),
        compiler_params=pltpu.CompilerParams(
            dimension_semantics=("arbitrary", "arbitrary", "arbitrary"), vmem_limit_bytes=_vmem_limit(need)),
        name="flash_diff",
    )(q_t, k, v_t, par_t)


def _decode_kernel(pt_ref, cm_hbm, ck_hbm, cv_hbm, qa_ref, qd_ref, rown_ref, kn_ref, vn_ref, wuk_ref,
                   om_ref, od_ref,
                   mbuf, kbuf, vbuf, sems, mm_ref, lm_ref, am_ref, md_ref, ld_ref, ad_ref,
                   *, page, pages_per_step, page_base, sample_len):
    b = pl.program_id(0)
    c = pl.program_id(1)
    nchunk = pl.num_programs(1)
    step = b * nchunk + c
    slot = step % 2
    tok = page * pages_per_step
    ncols = MLA_HEADS * sample_len

    def copies(bb, cc, sl):
        out = []
        for j in range(pages_per_step):
            pg = pt_ref[bb, cc * pages_per_step + j] + page_base
            out.append(pltpu.make_async_copy(
                cm_hbm.at[pl.ds(pl.multiple_of(pg * MLA_ROW, MLA_ROW), MLA_ROW)],
                mbuf.at[sl, :, pl.ds(j * page, page)], sems.at[sl, 0]))
            rows = pl.ds(pl.multiple_of(pg * (page * DIFF_HEADS), page * DIFF_HEADS), page * DIFF_HEADS)
            dst = pl.ds(j * page * DIFF_HEADS, page * DIFF_HEADS)
            out.append(pltpu.make_async_copy(ck_hbm.at[rows], kbuf.at[sl, dst], sems.at[sl, 1]))
            out.append(pltpu.make_async_copy(cv_hbm.at[rows], vbuf.at[sl, dst], sems.at[sl, 2]))
        return out

    @pl.when(step == 0)
    def _():
        for cp in copies(0, 0, 0):
            cp.start()

    @pl.when(step + 1 < pl.num_programs(0) * nchunk)
    def _():
        nxt = step + 1
        for cp in copies(nxt // nchunk, nxt % nchunk, 1 - slot):
            cp.start()

    @pl.when(c == 0)
    def _():
        mm_ref[...] = jnp.full(mm_ref.shape, NEG, F32)
        md_ref[...] = jnp.full(md_ref.shape, NEG, F32)
        lm_ref[...] = jnp.zeros(lm_ref.shape, F32)
        ld_ref[...] = jnp.zeros(ld_ref.shape, F32)
        am_ref[...] = jnp.zeros(am_ref.shape, F32)
        ad_ref[...] = jnp.zeros(ad_ref.shape, F32)

    for cp in copies(b, c, slot):
        cp.wait()

    def online(s, pv_fn, m_ref, l_ref, a_ref):
        m_old = m_ref[...]
        m_new = jnp.maximum(m_old, jnp.max(s, axis=1, keepdims=True))
        p = jnp.exp2(s - m_new)
        corr = jnp.exp2(m_old - m_new)
        l_ref[...] = l_ref[...] * corr + jnp.sum(p, axis=1, keepdims=True)
        a_ref[...] = a_ref[...] * corr + pv_fn(p.astype(BF16))
        m_ref[...] = m_new

    def mla_scores(row_t):
        n = row_t.shape[1]
        rowb = row_t.astype(BF16)
        lhs = jnp.concatenate([wuk_ref[...], qa_ref[0]], axis=0)
        y = _dot(lhs, rowb)
        kn = y[:MLA_HEADS * MLA_NOPE]
        ss = jnp.sum((kn * kn).reshape(MLA_NOPE, MLA_HEADS, n), axis=0)
        kr = row_t[KV_LORA:MLA_ROW]
        ss = ss + jnp.sum(kr * kr, axis=0, keepdims=True)
        rs = lax.rsqrt(ss * (1.0 / MLA_QK) + EPS)
        s = y[MLA_HEADS * MLA_NOPE:] * jnp.concatenate([rs] * sample_len, axis=0)
        return s, rowb[:KV_LORA]

    def heads_of(ref_fn):
        return jnp.concatenate([ref_fn(hh) for hh in range(DIFF_HEADS)], axis=-1).astype(BF16)

    kc = heads_of(lambda hh: kbuf[slot, pl.ds(hh, tok, stride=DIFF_HEADS), :])
    s_d = _dot_nt(qd_ref[0], kc)
    s_m, cb = mla_scores(mbuf[slot])
    vc = heads_of(lambda hh: vbuf[slot, pl.ds(hh, tok, stride=DIFF_HEADS), :])
    online(s_d, lambda p: _dot(p, vc), md_ref, ld_ref, ad_ref)
    online(s_m, lambda p: _dot_nt(p, cb), mm_ref, lm_ref, am_ref)

    @pl.when(c == nchunk - 1)
    def _():
        def mask(s, t_of_row):
            col = lax.broadcasted_iota(jnp.int32, s.shape, 1)
            row = lax.broadcasted_iota(jnp.int32, s.shape, 0)
            return jnp.where(col <= t_of_row(row), s, NEG)

        sn_m, cnb = mla_scores(rown_ref[0])
        online(mask(sn_m, lambda r: r // MLA_HEADS), lambda p: _dot_nt(p, cnb), mm_ref, lm_ref, am_ref)
        knb = kn_ref[0].astype(BF16)
        vnb = vn_ref[0].astype(BF16)
        sn_d = _dot_nt(qd_ref[0], knb)
        online(mask(sn_d, lambda r: r % sample_len), lambda p: _dot(p, vnb), md_ref, ld_ref, ad_ref)
        om_ref[0] = am_ref[...] / lm_ref[...]
        l_d = ld_ref[...]
        rows_per_head = 2 * sample_len
        for hh in range(DIFF_HEADS):
            rsl = slice(hh * rows_per_head, (hh + 1) * rows_per_head)
            od_ref[0, rsl, :] = ad_ref[rsl, hh * LANE:(hh + 1) * LANE] / l_d[rsl]


def _decode_call(page_table, cm_rows, ck_rows, cv_rows, qa, qd, row_new, k_new, v_new, wuk, page, page_base):
    bsz, n_pages = page_table.shape
    pps = min(DECODE_PAGES, n_pages)
    assert n_pages % pps == 0
    nchunk = n_pages // pps
    ncols = qa.shape[1]
    sample_len = ncols // MLA_HEADS
    tok = page * pps
    per_b = lambda shape: pl.BlockSpec((1,) + shape, lambda b, c, pt: (b, 0, 0))
    grid_spec = pltpu.PrefetchScalarGridSpec(
        num_scalar_prefetch=1,
        grid=(bsz, nchunk),
        in_specs=[pl.BlockSpec(memory_space=pl.ANY)] * 3 + [
            per_b((ncols, MLA_ROW)), per_b((ncols, DIFF_W)),
            per_b((MLA_ROW, page)), per_b((NEW_PAD, DIFF_W)), per_b((NEW_PAD, DIFF_W)),
            pl.BlockSpec(wuk.shape, lambda b, c, pt: (0, 0))],
        out_specs=[per_b((ncols, LANE)), per_b((ncols, LANE))],
        scratch_shapes=[
            pltpu.VMEM((2, MLA_ROW, tok), F32),
            pltpu.VMEM((2, tok * DIFF_HEADS, LANE), F32),
            pltpu.VMEM((2, tok * DIFF_HEADS, LANE), F32),
            pltpu.SemaphoreType.DMA((2, 3)),
            pltpu.VMEM((ncols, 1), F32), pltpu.VMEM((ncols, 1), F32), pltpu.VMEM((ncols, KV_LORA), F32),
            pltpu.VMEM((ncols, 1), F32), pltpu.VMEM((ncols, 1), F32), pltpu.VMEM((ncols, DIFF_W), F32),
        ])
    need = 2 * tok * (MLA_ROW + 2 * DIFF_W) * 4 + tok * (2 * (DIFF_W + MLA_ROW) * 2 + (MLA_HEADS * MLA_NOPE + 4 * ncols) * 4 * 2)
    return pl.pallas_call(
        functools.partial(_decode_kernel, page=page, pages_per_step=pps, page_base=page_base,
                          sample_len=sample_len),
        grid_spec=grid_spec,
        out_shape=[jax.ShapeDtypeStruct((bsz, ncols, LANE), F32)] * 2,
        compiler_params=pltpu.CompilerParams(
            dimension_semantics=("arbitrary", "arbitrary"), vmem_limit_bytes=_vmem_limit(need)),
        name="decode",
    )(page_table, cm_rows, ck_rows, cv_rows, qa, qd, row_new, k_new, v_new, wuk)


def _sample_combine_kernel(om_ref, od_ref, wuv_ref, par_ref, cm_ref, cd_ref):
    for hh in range(MLA_HEADS):
        cm_ref[:, hh * LANE:(hh + 1) * LANE] = _dot(om_ref[hh].astype(BF16), wuv_ref[hh]).astype(BF16)
    for hh in range(DIFF_HEADS):
        cd_ref[:, hh * LANE:(hh + 1) * LANE] = _diff_finish(od_ref[hh, 0], od_ref[hh, 1], par_ref[...]).astype(BF16)


def _sample_combine_call(om, od, wuv, par):
    n = om.shape[1]
    full = lambda a: pl.BlockSpec(a.shape, lambda i: (0,) * a.ndim)
    return pl.pallas_call(
        _sample_combine_kernel,
        grid=(1,),
        in_specs=[full(om), full(od), full(wuv), full(par)],
        out_specs=[pl.BlockSpec((n, MLA_W), lambda i: (0, 0)), pl.BlockSpec((n, DIFF_W), lambda i: (0, 0))],
        out_shape=[jax.ShapeDtypeStruct((n, MLA_W), BF16), jax.ShapeDtypeStruct((n, DIFF_W), BF16)],
        name="sample_combine",
    )(om, od, wuv, par)


def _rot_half_cols(w, group, half):
    w3 = w.reshape(w.shape[0], -1, group)
    zeros = jnp.zeros_like(w3[..., 2 * half:])
    return jnp.concatenate([-w3[..., half:2 * half], w3[..., :half], zeros], axis=-1).reshape(w.shape)


def _rot_half_gain(g, half):
    return jnp.concatenate([g[half:2 * half], g[:half], jnp.zeros_like(g[2 * half:])])


def _prep_layer(w_in, mix_norm, mla_q_norm, mla_kv_norm, w_uq, w_uk, w_uv, mla_q_gain, mla_k_gain,
                diff_q_gain, diff_k_gain):
    o1 = Q_LORA
    o2 = o1 + KV_LORA
    o3 = o2 + MLA_ROPE
    o4 = o3 + DIFF_W
    o5 = o4 + DIFF_W
    wkr = w_in[:, o2:o3]
    wdq, wdk, wdv = w_in[:, o3:o4], w_in[:, o4:o5], w_in[:, o5:]
    pad = jnp.zeros((D_MODEL, LANE - MLA_ROPE), F32)
    w_in_p = jnp.concatenate(
        [w_in[:, :o1], w_in[:, o1:o2], wdq, wdk, wdv,
         _rot_half_cols(wdq, DIFF_D, DIFF_ROT // 2), _rot_half_cols(wdk, DIFF_D, DIFF_ROT // 2),
         wkr, pad, _rot_half_cols(wkr, MLA_ROPE, MLA_ROPE // 2), pad], axis=1).astype(BF16)
    assert w_in_p.shape[1] == IN_COLS_P

    hpad = jnp.zeros((Q_LORA, MLA_HEADS, LANE - MLA_QK), F32)
    wq = jnp.concatenate([w_uq, hpad], axis=-1).reshape(Q_LORA, MLA_W)
    wq_rope = w_uq[..., MLA_NOPE:]
    wq_sw = jnp.concatenate(
        [jnp.zeros((Q_LORA, MLA_HEADS, MLA_NOPE), F32),
         _rot_half_cols(wq_rope.reshape(Q_LORA, -1), MLA_ROPE, MLA_ROPE // 2).reshape(Q_LORA, MLA_HEADS, MLA_ROPE),
         hpad], axis=-1).reshape(Q_LORA, MLA_W)
    wq2 = jnp.concatenate([wq, wq_sw], axis=1).astype(BF16)

    vpad = jnp.zeros((KV_LORA, MLA_HEADS, LANE - MLA_NOPE), F32)
    wk_c = jnp.concatenate([w_uk, vpad], axis=-1).reshape(KV_LORA, MLA_W)
    place = jnp.concatenate([jnp.zeros((MLA_ROPE, MLA_NOPE), F32), jnp.eye(MLA_ROPE, dtype=F32),
                             jnp.zeros((MLA_ROPE, LANE - MLA_QK), F32)], axis=1)
    wk_r = jnp.tile(place, (1, MLA_HEADS))
    wk = jnp.concatenate([wk_c, wk_r, jnp.zeros((LANE - MLA_ROPE, MLA_W), F32)], axis=0)
    wv = jnp.concatenate([jnp.concatenate([w_uv, vpad], axis=-1).reshape(KV_LORA, MLA_W),
                          jnp.zeros((LANE, MLA_W), F32)], axis=0)
    wkv = jnp.concatenate([wk, wv], axis=1).astype(BF16)

    wabs_top = jnp.concatenate([jnp.transpose(w_uk, (1, 2, 0)), jnp.zeros((MLA_HEADS, MLA_NOPE, LANE), F32)], axis=-1)
    wabs_mid = jnp.concatenate([jnp.zeros((MLA_ROPE, LANE), F32), jnp.eye(MLA_ROPE, dtype=F32),
                                jnp.zeros((MLA_ROPE, LANE - MLA_ROPE), F32)], axis=1)
    wabs = jnp.concatenate([wabs_top, jnp.broadcast_to(wabs_mid, (MLA_HEADS,) + wabs_mid.shape),
                            jnp.zeros((MLA_HEADS, LANE - MLA_QK, 2 * LANE), F32)], axis=1).astype(BF16)

    wuk_t = jnp.concatenate([jnp.transpose(w_uk, (2, 1, 0)).reshape(MLA_HEADS * MLA_NOPE, KV_LORA),
                             jnp.zeros((MLA_HEADS * MLA_NOPE, MLA_ROPE), F32)], axis=1).astype(BF16)
    wuv_pad = jnp.concatenate([jnp.transpose(w_uv, (1, 0, 2)), jnp.zeros((MLA_HEADS, KV_LORA, LANE - MLA_V), F32)],
                              axis=-1).astype(BF16)

    qpad = jnp.zeros((LANE - MLA_QK,), F32)
    row1 = jnp.concatenate([mla_q_norm, mla_kv_norm, mla_q_gain, qpad, mla_k_gain, qpad,
                            jnp.zeros((D_MODEL - Q_LORA - KV_LORA - 2 * LANE,), F32)])
    reps = DIFF_W // DIFF_D
    row2 = jnp.concatenate([jnp.tile(diff_q_gain, reps), jnp.tile(_rot_half_gain(diff_q_gain, DIFF_ROT // 2), reps)])
    row3 = jnp.concatenate([jnp.tile(diff_k_gain, reps), jnp.tile(_rot_half_gain(diff_k_gain, DIFF_ROT // 2), reps)])
    vec = jnp.stack([mix_norm, row1, row2, row3] + [jnp.zeros((D_MODEL,), F32)] * 4)
    b64 = jnp.kron(jnp.eye(DIFF_W // DIFF_D, dtype=F32), jnp.ones((DIFF_D, DIFF_D), F32)).astype(BF16)
    return dict(w_in=w_in_p, wq=wq2, wkv=wkv, wabs=wabs, wuk_t=wuk_t, wuv_pad=wuv_pad, vec=vec, b64=b64)


def _rope_tables(pos):
    n = pos.shape[0]

    def cs(r):
        half = r // 2
        inv = jnp.power(jnp.float32(ROPE_THETA), -jnp.arange(half, dtype=F32) * (2.0 / r))
        ang = pos.astype(F32)[:, None] * inv[None, :]
        return jnp.cos(ang), jnp.sin(ang)

    cm, sm = cs(MLA_ROPE)
    cd, sd = cs(DIFF_ROT)
    ones = lambda w: jnp.ones((n, w), F32)
    zeros = lambda w: jnp.zeros((n, w), F32)
    c64 = jnp.concatenate([cd, cd, ones(DIFF_D - DIFF_ROT)], axis=1)
    s64 = jnp.concatenate([sd, sd, zeros(DIFF_D - DIFF_ROT)], axis=1)
    return jnp.concatenate([
        ones(MLA_NOPE), cm, cm, zeros(LANE - MLA_QK),
        zeros(MLA_NOPE), sm, sm, zeros(LANE - MLA_QK),
        cm, cm, zeros(LANE - MLA_ROPE),
        sm, sm, zeros(LANE - MLA_ROPE),
        c64, c64, s64, s64], axis=1)


def kernel(x_prompt, x_sample, cache_mla, cache_diff_k, cache_diff_v, page_table, ffn1_norm, ffn1_w_gate, ffn1_w_up, ffn1_w_down, mix_norm, w_in, mla_q_norm, mla_kv_norm, w_uq, w_uk, w_uv, mla_q_gain, mla_k_gain, diff_q_gain, diff_k_gain, diff_lq1, diff_lk1, diff_lq2, diff_lk2, diff_sub_gain, w_o, ffn2_norm, ffn2_w_gate, ffn2_w_up, ffn2_w_down):
    depth = w_in.shape[0]
    bp, seq, _ = x_prompt.shape
    bs, ts, _ = x_sample.shape
    n_pool, page = cache_mla.shape[1], cache_mla.shape[2]
    past_len = page_table.shape[1] * page
    assert ts <= NEW_PAD and MLA_HEADS * ts == 2 * DIFF_HEADS * ts

    cm_rows = jnp.swapaxes(cache_mla, 2, 3).reshape(-1, page)
    ck_rows = cache_diff_k.reshape(-1, DIFF_HD)
    cv_rows = cache_diff_v.reshape(-1, DIFF_HD)

    tab_p = _rope_tables(jnp.arange(seq, dtype=jnp.int32))
    tab_s = jnp.tile(_rope_tables(past_len + jnp.arange(ts, dtype=jnp.int32)), (bs, 1))

    yp = x_prompt.reshape(bp * seq, D_MODEL)
    ys = x_sample.reshape(bs * ts, D_MODEL)
    outs = [[] for _ in range(6)]
    for l in range(depth):
        lp = _prep_layer(w_in[l], mix_norm[l], mla_q_norm[l], mla_kv_norm[l], w_uq[l], w_uk[l], w_uv[l],
                         mla_q_gain[l], mla_k_gain[l], diff_q_gain[l], diff_k_gain[l])
        lam_init = 0.8 - 0.6 * math.exp(-0.3 * l)
        lam = (jnp.exp(jnp.sum(diff_lq1[l] * diff_lk1[l])) - jnp.exp(jnp.sum(diff_lq2[l] * diff_lk2[l])) + lam_init)
        par = jnp.stack([jnp.broadcast_to(lam, (DIFF_HD,)), diff_sub_gain[l] * (1.0 - lam_init)]
                        + [jnp.zeros((DIFF_HD,), F32)] * 6)
        f1 = (ffn1_w_gate[l].astype(BF16), ffn1_w_up[l].astype(BF16), ffn1_w_down[l].astype(BF16))
        f2 = (ffn2_w_gate[l].astype(BF16), ffn2_w_up[l].astype(BF16), ffn2_w_down[l].astype(BF16))
        wo_m = jnp.concatenate([w_o[l, :MLA_HEADS * MLA_V].reshape(MLA_HEADS, MLA_V, D_MODEL),
                                jnp.zeros((MLA_HEADS, LANE - MLA_V, D_MODEL), F32)], axis=1
                               ).reshape(MLA_W, D_MODEL).astype(BF16)
        wo_d = w_o[l, MLA_HEADS * MLA_V:].astype(BF16)

        x1p = _ffn_call(yp, ffn1_norm[l], *f1)
        row_p, dk_p, dv_p, qm, km, vm, qd, kd, vd = _proj_call(x1p, tab_p, lp, sample=False)
        cat_m = _flash_call(qm, km, vm, bp, seq, MLA_HEADS_PER_STEP)
        cat_d = _flash_diff_call(jnp.swapaxes(qd, 2, 3), kd, jnp.swapaxes(vd, 1, 2), jnp.transpose(par), bp, seq)
        yp = _ffn_call(x1p, ffn2_norm[l], *f2, attn=(cat_m, cat_d, wo_m, wo_d))

        x1s = _ffn_call(ys, ffn1_norm[l], *f1)
        row_s, dk_s, dv_s, qabs, dq_s = _proj_call(x1s, tab_s, lp, sample=True)
        qa = jnp.transpose(qabs[:, :, :MLA_ROW].reshape(MLA_HEADS, bs, ts, MLA_ROW), (1, 2, 0, 3))
        qa = qa.reshape(bs, ts * MLA_HEADS, MLA_ROW).astype(BF16)
        dq5 = dq_s.reshape(bs, ts, DIFF_HEADS, 2, DIFF_D)
        qbd = jnp.einsum("bthmd,hH,mM->bhmtHMd", dq5, jnp.eye(DIFF_HEADS, dtype=F32), jnp.eye(2, dtype=F32))
        qbd = qbd.reshape(bs, DIFF_HEADS * 2 * ts, DIFF_W).astype(BF16)
        padn = lambda a: jnp.pad(a.reshape(bs, ts, -1), ((0, 0), (0, NEW_PAD - ts), (0, 0)))
        row_new = jnp.pad(jnp.swapaxes(row_s.reshape(bs, ts, MLA_ROW), 1, 2), ((0, 0), (0, 0), (0, page - ts)))
        om, od = _decode_call(page_table, cm_rows, ck_rows, cv_rows, qa, qbd, row_new, padn(dk_s), padn(dv_s),
                              lp["wuk_t"], page, l * n_pool)
        om = jnp.transpose(om.reshape(bs, ts, MLA_HEADS, KV_LORA), (2, 0, 1, 3)).reshape(MLA_HEADS, bs * ts, KV_LORA)
        od = jnp.transpose(od.reshape(bs, DIFF_HEADS, 2, ts, DIFF_HD), (1, 2, 0, 3, 4)
                           ).reshape(DIFF_HEADS, 2, bs * ts, DIFF_HD)
        cat_ms, cat_ds = _sample_combine_call(om, od, lp["wuv_pad"], par)
        ys = _ffn_call(x1s, ffn2_norm[l], *f2, attn=(cat_ms, cat_ds, wo_m, wo_d))

        outs[0].append(row_p.reshape(bp, seq, MLA_ROW))
        outs[1].append(dk_p.reshape(bp, seq, DIFF_HEADS, DIFF_HD))
        outs[2].append(dv_p.reshape(bp, seq, DIFF_HEADS, DIFF_HD))
        outs[3].append(row_s.reshape(bs, ts, MLA_ROW))
        outs[4].append(dk_s.reshape(bs, ts, DIFF_HEADS, DIFF_HD))
        outs[5].append(dv_s.reshape(bs, ts, DIFF_HEADS, DIFF_HD))
    return (yp.reshape(bp, seq, D_MODEL), ys.reshape(bs, ts, D_MODEL)) + tuple(jnp.stack(o) for o in outs)
```
